```python
import math
import jax, jax.numpy as jnp
from jax import lax
import numpy as np

D_MODEL = 1024
BATCH = 2
SEQ = 16384
DEPTH = 1

N_MEM = 256
D_FF = 2816
D_CONV = D_MODEL
CONV_WIDTH = 31
D_SGU = D_MODEL
SGU_GROUPS = 4
CHUNK = 128
X_HEADS = 4
X_HEAD_DIM = D_MODEL // X_HEADS
D_IN = 2 * D_CONV + 2 * D_SGU + 2 * D_MODEL
EPS_RMS = 1e-6
EPS_LN = 1e-5

kernel_name = "hybrid_conformer_gmlp_memxattn_block"


def rms_norm(x, g):
    xf = x.astype(jnp.float32)
    y = xf * lax.rsqrt(jnp.mean(xf * xf, axis=-1, keepdims=True) + EPS_RMS)
    return (y * g.astype(jnp.float32)).astype(x.dtype)


def layer_norm(x, g, b):
    xf = x.astype(jnp.float32)
    mu = jnp.mean(xf, axis=-1, keepdims=True)
    xc = xf - mu
    var = jnp.mean(xc * xc, axis=-1, keepdims=True)
    y = xc * lax.rsqrt(var + EPS_LN)
    return (y * g.astype(jnp.float32) + b.astype(jnp.float32)).astype(x.dtype)


def swiglu(x, w_gu, w_down):
    gu = x @ w_gu
    g, u = jnp.split(gu, 2, axis=-1)
    return (jax.nn.silu(g) * u) @ w_down


def causal_depthwise_conv(a, w, b):
    c = a.shape[-1]
    y = lax.conv_general_dilated(
        a, w.astype(a.dtype)[:, None, :],
        window_strides=(1,), padding=[(CONV_WIDTH - 1, 0)],
        dimension_numbers=("NWC", "WIO", "NWC"),
        feature_group_count=c)
    return y + b.astype(a.dtype)


def conformer_conv_branch(a_val, a_gate, conv_w, conv_b, ln_g, ln_b, w_a_out):
    a = a_val * jax.nn.sigmoid(a_gate)
    a = causal_depthwise_conv(a, conv_w, conv_b)
    a = jax.nn.silu(layer_norm(a, ln_g, ln_b))
    return a @ w_a_out


def spatial_gating_branch(u, v, ln_g, ln_b, sgu_w, sgu_b, w_b_out):
    bsz, s, _ = u.shape
    u = jax.nn.gelu(u)
    v = layer_norm(jax.nn.gelu(v), ln_g, ln_b)
    n_chunks = s // CHUNK
    gd = D_SGU // SGU_GROUPS
    vc = v.reshape(bsz, n_chunks, CHUNK, SGU_GROUPS, gd)
    mask = jnp.tril(jnp.ones((CHUNK, CHUNK), dtype=bool))
    w_s = jnp.where(mask[None], sgu_w, 0.0).astype(v.dtype)
    mixed = jnp.einsum("gts,bcsgd->bctgd", w_s, vc)
    mixed = mixed + jnp.transpose(sgu_b)[None, None, :, :, None].astype(v.dtype)
    out = u * mixed.reshape(bsz, s, D_SGU)
    return out @ w_b_out


def memory_cross_attention(xn, memn, w_q, w_kv, w_o):
    bsz, s, _ = xn.shape
    q = (xn @ w_q).reshape(bsz, s, X_HEADS, X_HEAD_DIM)
    kv = memn @ w_kv
    k, v = jnp.split(kv, 2, axis=-1)
    k = k.reshape(bsz, N_MEM, X_HEADS, X_HEAD_DIM)
    v = v.reshape(bsz, N_MEM, X_HEADS, X_HEAD_DIM)
    scores = jnp.einsum("bshd,bmhd->bhsm", q.astype(jnp.float32), k.astype(jnp.float32))
    p = jax.nn.softmax(scores * (1.0 / math.sqrt(X_HEAD_DIM)), axis=-1).astype(v.dtype)
    o = jnp.einsum("bhsm,bmhd->bshd", p, v).reshape(bsz, s, D_MODEL)
    return o @ w_o


def setup_inputs(seed: int = 0) -> dict:
    key = jax.random.key(seed)
    ks = jax.random.split(key, 32)

    def dense(k, shape, fan_in):
        return jax.random.normal(k, shape, jnp.float32) * (fan_in ** -0.5)

    def gain(k, shape):
        return 1.0 + 0.02 * jax.random.normal(k, shape, jnp.float32)

    def small(k, shape):
        return 0.02 * jax.random.normal(k, shape, jnp.float32)

    L = DEPTH
    return {
        "x": jax.random.normal(ks[0], (BATCH, SEQ, D_MODEL), jnp.float32),
        "mem": jax.random.normal(ks[1], (BATCH, N_MEM, D_MODEL), jnp.float32),
        "ffn1_norm": gain(ks[2], (L, D_MODEL)),
        "ffn1_w_gu": dense(ks[3], (L, D_MODEL, 2 * D_FF), D_MODEL),
        "ffn1_w_down": dense(ks[4], (L, D_FF, D_MODEL), D_FF),
        "mix_norm": gain(ks[5], (L, D_MODEL)),
        "w_in": dense(ks[6], (L, D_MODEL, D_IN), D_MODEL),
        "b_in": small(ks[7], (L, D_IN)),
        "conv_w": dense(ks[8], (L, CONV_WIDTH, D_CONV), CONV_WIDTH),
        "conv_b": small(ks[9], (L, D_CONV)),
        "conv_ln_g": gain(ks[10], (L, D_CONV)),
        "conv_ln_b": small(ks[11], (L, D_CONV)),
        "w_a_out": dense(ks[12], (L, D_CONV, D_MODEL), D_CONV),
        "sgu_ln_g": gain(ks[13], (L, D_SGU)),
        "sgu_ln_b": small(ks[14], (L, D_SGU)),
        "sgu_w": dense(ks[15], (L, SGU_GROUPS, CHUNK, CHUNK), CHUNK),
        "sgu_b": gain(ks[16], (L, SGU_GROUPS, CHUNK)),
        "w_b_out": dense(ks[17], (L, D_SGU, D_MODEL), D_SGU),
        "w_out": dense(ks[18], (L, D_MODEL, D_MODEL), D_MODEL),
        "xattn_norm": gain(ks[19], (L, D_MODEL)),
        "mem_norm": gain(ks[20], (L, D_MODEL)),
        "w_q": dense(ks[21], (L, D_MODEL, D_MODEL), D_MODEL),
        "w_kv": dense(ks[22], (L, D_MODEL, 2 * D_MODEL), D_MODEL),
        "w_o": dense(ks[23], (L, D_MODEL, D_MODEL), D_MODEL),
        "ffn2_norm": gain(ks[24], (L, D_MODEL)),
        "ffn2_w_gu": dense(ks[25], (L, D_MODEL, 2 * D_FF), D_MODEL),
        "ffn2_w_down": dense(ks[26], (L, D_FF, D_MODEL), D_FF),
        "final_norm": gain(ks[27], (D_MODEL,)),
    }


def reference(x, mem, ffn1_norm, ffn1_w_gu, ffn1_w_down, mix_norm, w_in, b_in,
              conv_w, conv_b, conv_ln_g, conv_ln_b, w_a_out,
              sgu_ln_g, sgu_ln_b, sgu_w, sgu_b, w_b_out, w_out,
              xattn_norm, mem_norm, w_q, w_kv, w_o,
              ffn2_norm, ffn2_w_gu, ffn2_w_down, final_norm):
    split_at = [D_CONV, 2 * D_CONV, 2 * D_CONV + D_SGU, 2 * D_CONV + 2 * D_SGU,
                2 * D_CONV + 2 * D_SGU + D_MODEL]
    h = x
    for l in range(DEPTH):
        h = h + 0.5 * swiglu(rms_norm(h, ffn1_norm[l]), ffn1_w_gu[l], ffn1_w_down[l])

        n = rms_norm(h, mix_norm[l])
        p = n @ w_in[l] + b_in[l]
        a_val, a_gate, b_u, b_v, g_a, g_b = jnp.split(p, split_at, axis=-1)
        y_a = conformer_conv_branch(a_val, a_gate, conv_w[l], conv_b[l],
                                    conv_ln_g[l], conv_ln_b[l], w_a_out[l])
        y_b = spatial_gating_branch(b_u, b_v, sgu_ln_g[l], sgu_ln_b[l],
                                    sgu_w[l], sgu_b[l], w_b_out[l])
        merged = jax.nn.sigmoid(g_a) * y_a + jax.nn.sigmoid(g_b) * y_b
        h = h + merged @ w_out[l]

        h = h + memory_cross_attention(rms_norm(h, xattn_norm[l]), rms_norm(mem, mem_norm[l]),
                                       w_q[l], w_kv[l], w_o[l])

        h = h + 0.5 * swiglu(rms_norm(h, ffn2_norm[l]), ffn2_w_gu[l], ffn2_w_down[l])
    return rms_norm(h, final_norm)
```

```python
import functools
import math

import jax
import jax.numpy as jnp
from jax import lax
from jax.experimental import pallas as pl
from jax.experimental.pallas import tpu as pltpu

D_MODEL = 1024
N_MEM = 256
D_FF = 2816
CONV_WIDTH = 31
SGU_GROUPS = 4
CHUNK = 128
X_HEADS = 4
X_HEAD_DIM = D_MODEL // X_HEADS
GROUP_DIM = D_MODEL // SGU_GROUPS
D_IN = 6 * D_MODEL
EPS_RMS = 1e-6
EPS_LN = 1e-5

F32 = jnp.float32
BF16 = jnp.bfloat16

V7X_VMEM_BYTES = 64 * 1024 * 1024
VMEM_LIMIT_BYTES = V7X_VMEM_BYTES - 8 * 1024 * 1024

FFN_TILE = 512
MIX_TILE = 256
FF_CHUNK = 256
CONV_HALO = 32
CONV_ROWS = 32


def _dot(a, b):
    return jnp.dot(a, b, preferred_element_type=F32)


def _rms(x, g):
    ms = jnp.mean(x * x, axis=-1, keepdims=True)
    return x * lax.rsqrt(ms + EPS_RMS) * g


def _layer_norm(x, g, b):
    mu = jnp.mean(x, axis=-1, keepdims=True)
    xc = x - mu
    var = jnp.mean(xc * xc, axis=-1, keepdims=True)
    return xc * lax.rsqrt(var + EPS_LN) * g + b


def _gelu_tanh(x):
    c = math.sqrt(2.0 / math.pi)
    return x * (0.5 * (1.0 + jnp.tanh(c * (x + 0.044715 * (x * x * x)))))


def _swiglu_half_step(x, g_ref, wg_ref, wu_ref, wd_ref, act_ref):
    xn = _rms(x, g_ref[...]).astype(BF16)
    for c in range(D_FF // FF_CHUNK):
        sl = slice(c * FF_CHUNK, (c + 1) * FF_CHUNK)
        gate = _dot(xn, wg_ref[:, sl])
        up = _dot(xn, wu_ref[:, sl])
        act_ref[:, sl] = (gate * jax.nn.sigmoid(gate) * up).astype(BF16)
    return x + 0.5 * _dot(act_ref[...], wd_ref[...])


def _kv_kernel(mem_ref, g_ref, wkv_ref, kt_ref, v_ref):
    memn = _rms(mem_ref[...], g_ref[...]).astype(BF16)
    kv = _dot(memn, wkv_ref[...])
    kt_ref[...] = kv[:, :D_MODEL].T.astype(BF16)
    v_ref[...] = kv[:, D_MODEL:].astype(BF16)


def _ffn_kernel(x_ref, g_ref, wg_ref, wu_ref, wd_ref, o_ref, act_ref):
    o_ref[...] = _swiglu_half_step(x_ref[...], g_ref, wg_ref, wu_ref, wd_ref, act_ref)


def _mixer_kernel(h_ref, g_ref, win_ref, bin_ref, cw_ref, cb_ref, clg_ref, clb_ref, wa_ref,
                  slg_ref, slb_ref, sw_ref, sbias_ref, wb_ref, wo_ref, o_ref,
                  n_ref, aext_ref, za_ref, m_ref, v_ref, gated_ref):
    tile = h_ref.shape[0]

    def proj(col):
        sl = slice(col * D_MODEL, (col + 1) * D_MODEL)
        return _dot(n_ref[...], win_ref[:, sl]) + bin_ref[:, sl]

    @pl.when(pl.program_id(1) == 0)
    def _():
        aext_ref[0:CONV_HALO, :] = jnp.zeros((CONV_HALO, D_MODEL), F32)

    n_ref[...] = _rms(h_ref[...], g_ref[...]).astype(BF16)

    aext_ref[CONV_HALO:, :] = proj(0) * jax.nn.sigmoid(proj(1))
    for r in range(tile // CONV_ROWS):
        base = r * CONV_ROWS
        acc = jnp.broadcast_to(cb_ref[...], (CONV_ROWS, D_MODEL))
        for k in range(CONV_WIDTH):
            start = base + CONV_HALO - (CONV_WIDTH - 1) + k
            acc = acc + cw_ref[k:k + 1, :] * aext_ref[start:start + CONV_ROWS, :]
        z = _layer_norm(acc, clg_ref[...], clb_ref[...])
        za_ref[base:base + CONV_ROWS, :] = (z * jax.nn.sigmoid(z)).astype(BF16)
    aext_ref[0:CONV_HALO, :] = aext_ref[tile:tile + CONV_HALO, :]
    m_ref[...] = jax.nn.sigmoid(proj(4)) * _dot(za_ref[...], wa_ref[...])

    v_ref[...] = _layer_norm(_gelu_tanh(proj(3)), slg_ref[...], slb_ref[...]).astype(BF16)
    row = lax.broadcasted_iota(jnp.int32, (CHUNK, CHUNK), 0)
    col = lax.broadcasted_iota(jnp.int32, (CHUNK, CHUNK), 1)
    w_s = [jnp.where(row >= col, sw_ref[g], 0.0).astype(BF16) for g in range(SGU_GROUPS)]
    u = _gelu_tanh(proj(2))
    for c in range(tile // CHUNK):
        rows = slice(c * CHUNK, (c + 1) * CHUNK)
        for g in range(SGU_GROUPS):
            cols = slice(g * GROUP_DIM, (g + 1) * GROUP_DIM)
            mixed = _dot(w_s[g], v_ref[rows, cols]) + sbias_ref[:, cols]
            gated_ref[rows, cols] = (u[rows, cols] * mixed).astype(BF16)
    m_ref[...] += jax.nn.sigmoid(proj(5)) * _dot(gated_ref[...], wb_ref[...])

    o_ref[...] = h_ref[...] + _dot(m_ref[...].astype(BF16), wo_ref[...])


def _tail_kernel(h_ref, xg_ref, wq_ref, kt_ref, v_ref, wo_ref,
                 fg_ref, wg_ref, wu_ref, wd_ref, og_ref, o_ref, attn_ref, act_ref):
    h = h_ref[...]
    q = _dot(_rms(h, xg_ref[...]).astype(BF16), wq_ref[...]).astype(BF16)
    scale = 1.0 / math.sqrt(X_HEAD_DIM)
    for hd in range(X_HEADS):
        cols = slice(hd * X_HEAD_DIM, (hd + 1) * X_HEAD_DIM)
        s = _dot(q[:, cols], kt_ref[cols, :]) * scale
        e = jnp.exp(s - jnp.max(s, axis=-1, keepdims=True))
        p = e / jnp.sum(e, axis=-1, keepdims=True)
        attn_ref[:, cols] = _dot(p.astype(BF16), v_ref[:, cols]).astype(BF16)
    h = h + _dot(attn_ref[...], wo_ref[...])
    h = _swiglu_half_step(h, fg_ref, wg_ref, wu_ref, wd_ref, act_ref)
    o_ref[...] = _rms(h, og_ref[...])


def _resident(shape):
    nd = len(shape)
    return pl.BlockSpec(shape, lambda *_: (0,) * nd, pipeline_mode=pl.Buffered(1))


def _params(semantics):
    return pltpu.CompilerParams(dimension_semantics=semantics, vmem_limit_bytes=VMEM_LIMIT_BYTES)


def _kv_call(mem, mem_norm, w_kv):
    bsz = mem.shape[0]
    return pl.pallas_call(
        _kv_kernel,
        grid=(bsz,),
        in_specs=[pl.BlockSpec((None, N_MEM, D_MODEL), lambda b: (b, 0, 0)),
                  _resident((1, D_MODEL)), _resident((D_MODEL, 2 * D_MODEL))],
        out_specs=[pl.BlockSpec((None, D_MODEL, N_MEM), lambda b: (b, 0, 0)),
                   pl.BlockSpec((None, N_MEM, D_MODEL), lambda b: (b, 0, 0))],
        out_shape=[jax.ShapeDtypeStruct((bsz, D_MODEL, N_MEM), BF16),
                   jax.ShapeDtypeStruct((bsz, N_MEM, D_MODEL), BF16)],
        compiler_params=_params(("arbitrary",)),
        name="kv_call",
    )(mem, mem_norm, w_kv)


def _ffn_call(x2d, norm, wg, wu, wd):
    tokens = x2d.shape[0]
    tok = pl.BlockSpec((FFN_TILE, D_MODEL), lambda i: (i, 0))
    return pl.pallas_call(
        _ffn_kernel,
        grid=(tokens // FFN_TILE,),
        in_specs=[tok, _resident((1, D_MODEL)), _resident((D_MODEL, D_FF)),
                  _resident((D_MODEL, D_FF)), _resident((D_FF, D_MODEL))],
        out_specs=tok,
        out_shape=jax.ShapeDtypeStruct(x2d.shape, F32),
        scratch_shapes=[pltpu.VMEM((FFN_TILE, D_FF), BF16)],
        compiler_params=_params(("arbitrary",)),
        name="ffn_call",
    )(x2d, norm, wg, wu, wd)


def _mixer_call(h, norm, w_in, b_in, conv_w, conv_b, cln_g, cln_b, w_a, sln_g, sln_b,
                sgu_w, sgu_bias, w_b, w_out):
    bsz, seq, _ = h.shape
    tok = pl.BlockSpec((None, MIX_TILE, D_MODEL), lambda b, j: (b, j, 0))
    vec = _resident((1, D_MODEL))
    mat = _resident((D_MODEL, D_MODEL))
    return pl.pallas_call(
        _mixer_kernel,
        grid=(bsz, seq // MIX_TILE),
        in_specs=[tok, vec, _resident((D_MODEL, D_IN)), _resident((1, D_IN)),
                  _resident((CONV_WIDTH, D_MODEL)), vec, vec, vec, mat,
                  vec, vec, _resident((SGU_GROUPS, CHUNK, CHUNK)), _resident((CHUNK, D_MODEL)),
                  mat, mat],
        out_specs=tok,
        out_shape=jax.ShapeDtypeStruct(h.shape, F32),
        scratch_shapes=[pltpu.VMEM((MIX_TILE, D_MODEL), BF16),
                        pltpu.VMEM((MIX_TILE + CONV_HALO, D_MODEL), F32),
                        pltpu.VMEM((MIX_TILE, D_MODEL), BF16),
                        pltpu.VMEM((MIX_TILE, D_MODEL), F32),
                        pltpu.VMEM((MIX_TILE, D_MODEL), BF16),
                        pltpu.VMEM((MIX_TILE, D_MODEL), BF16)],
        compiler_params=_params(("arbitrary", "arbitrary")),
        name="mixer_call",
    )(h, norm, w_in, b_in, conv_w, conv_b, cln_g, cln_b, w_a, sln_g, sln_b,
      sgu_w, sgu_bias, w_b, w_out)


def _tail_call(h, xnorm, w_q, kt, v, w_o, fnorm, wg, wu, wd, onorm):
    bsz, seq, _ = h.shape
    tok = pl.BlockSpec((None, FFN_TILE, D_MODEL), lambda b, j: (b, j, 0))
    vec = _resident((1, D_MODEL))
    mat = _resident((D_MODEL, D_MODEL))
    return pl.pallas_call(
        _tail_kernel,
        grid=(bsz, seq // FFN_TILE),
        in_specs=[tok, vec, mat,
                  pl.BlockSpec((None, D_MODEL, N_MEM), lambda b, j: (b, 0, 0)),
                  pl.BlockSpec((None, N_MEM, D_MODEL), lambda b, j: (b, 0, 0)),
                  mat, vec, _resident((D_MODEL, D_FF)), _resident((D_MODEL, D_FF)),
                  _resident((D_FF, D_MODEL)), vec],
        out_specs=tok,
        out_shape=jax.ShapeDtypeStruct(h.shape, F32),
        scratch_shapes=[pltpu.VMEM((FFN_TILE, D_MODEL), BF16),
                        pltpu.VMEM((FFN_TILE, D_FF), BF16)],
        compiler_params=_params(("arbitrary", "arbitrary")),
        name="tail_call",
    )(h, xnorm, w_q, kt, v, w_o, fnorm, wg, wu, wd, onorm)


def kernel(x, mem, ffn1_norm, ffn1_w_gu, ffn1_w_down, mix_norm, w_in, b_in, conv_w, conv_b, conv_ln_g, conv_ln_b, w_a_out, sgu_ln_g, sgu_ln_b, sgu_w, sgu_b, w_b_out, w_out, xattn_norm, mem_norm, w_q, w_kv, w_o, ffn2_norm, ffn2_w_gu, ffn2_w_down, final_norm):
    bsz, seq, _ = x.shape
    depth = ffn1_norm.shape[0]
    assert seq % MIX_TILE == 0 and seq % FFN_TILE == 0 and MIX_TILE % CHUNK == 0

    def vec(p):
        return p.reshape(1, -1)

    def bf(p):
        return p.astype(BF16)

    h = x
    for l in range(depth):
        h = _ffn_call(h.reshape(bsz * seq, D_MODEL), vec(ffn1_norm[l]),
                      bf(ffn1_w_gu[l][:, :D_FF]), bf(ffn1_w_gu[l][:, D_FF:]),
                      bf(ffn1_w_down[l])).reshape(bsz, seq, D_MODEL)
        sgu_bias = jnp.repeat(jnp.transpose(sgu_b[l]), GROUP_DIM, axis=1)
        h = _mixer_call(h, vec(mix_norm[l]), bf(w_in[l]), vec(b_in[l]), conv_w[l], vec(conv_b[l]),
                        vec(conv_ln_g[l]), vec(conv_ln_b[l]), bf(w_a_out[l]),
                        vec(sgu_ln_g[l]), vec(sgu_ln_b[l]), sgu_w[l], sgu_bias,
                        bf(w_b_out[l]), bf(w_out[l]))
        kt, v = _kv_call(mem, vec(mem_norm[l]), bf(w_kv[l]))
        last = l == depth - 1
        assert last, "the final norm is fused into the last layer's tail kernel"
        h = _tail_call(h, vec(xattn_norm[l]), bf(w_q[l]), kt, v, bf(w_o[l]),
                       vec(ffn2_norm[l]), bf(ffn2_w_gu[l][:, :D_FF]), bf(ffn2_w_gu[l][:, D_FF:]),
                       bf(ffn2_w_down[l]), vec(final_norm))
    return h
```

```python
import functools
import math

import jax
import jax.numpy as jnp
from jax import lax
from jax.experimental import pallas as pl
from jax.experimental.pallas import tpu as pltpu

D_MODEL = 1024
N_MEM = 256
D_FF = 2816
CONV_WIDTH = 31
SGU_GROUPS = 4
CHUNK = 128
X_HEADS = 4
X_HEAD_DIM = D_MODEL // X_HEADS
GROUP_DIM = D_MODEL // SGU_GROUPS
D_IN = 6 * D_MODEL
EPS_RMS = 1e-6
EPS_LN = 1e-5

F32 = jnp.float32
BF16 = jnp.bfloat16

V7X_VMEM_BYTES = 64 * 1024 * 1024
VMEM_LIMIT_BYTES = V7X_VMEM_BYTES - 8 * 1024 * 1024

FFN_TILE = 512
MIX_TILE = 256
FF_CHUNK = 256
CONV_HALO = 32
CONV_ROWS = 32
SUBLANES = 8


def _dot(a, b):
    return jnp.dot(a, b, preferred_element_type=F32)


def _rms(x, g):
    ms = jnp.mean(x * x, axis=-1, keepdims=True)
    return x * lax.rsqrt(ms + EPS_RMS) * g


def _layer_norm(x, g, b):
    mu = jnp.mean(x, axis=-1, keepdims=True)
    xc = x - mu
    var = jnp.mean(xc * xc, axis=-1, keepdims=True)
    return xc * lax.rsqrt(var + EPS_LN) * g + b


def _gelu_tanh(x):
    c = math.sqrt(2.0 / math.pi)
    return x * (0.5 * (1.0 + jnp.tanh(c * (x + 0.044715 * (x * x * x)))))


def _swiglu_half_step(x, g_ref, wg_ref, wu_ref, wd_ref, act_ref):
    xn = _rms(x, g_ref[...]).astype(BF16)
    for c in range(D_FF // FF_CHUNK):
        sl = slice(c * FF_CHUNK, (c + 1) * FF_CHUNK)
        gate = _dot(xn, wg_ref[:, sl])
        up = _dot(xn, wu_ref[:, sl])
        act_ref[:, sl] = (gate * jax.nn.sigmoid(gate) * up).astype(BF16)
    return x + 0.5 * _dot(act_ref[...], wd_ref[...])


def _kv_kernel(mem_ref, g_ref, wkv_ref, kt_ref, v_ref):
    memn = _rms(mem_ref[...], g_ref[...]).astype(BF16)
    kv = _dot(memn, wkv_ref[...])
    kt_ref[...] = kv[:, :D_MODEL].T.astype(BF16)
    v_ref[...] = kv[:, D_MODEL:].astype(BF16)


def _ffn_kernel(x_ref, g_ref, wg_ref, wu_ref, wd_ref, o_ref, act_ref):
    o_ref[...] = _swiglu_half_step(x_ref[...], g_ref, wg_ref, wu_ref, wd_ref, act_ref)


def _mixer_kernel(h_ref, g_ref, win_ref, bin_ref, cw_ref, cb_ref, clg_ref, clb_ref, wa_ref,
                  slg_ref, slb_ref, sw_ref, sbias_ref, wb_ref, wo_ref, o_ref,
                  n_ref, ext_ref, za_ref, t_ref, u_ref, ga_ref, gb_ref, v_ref, gated_ref):
    tile = h_ref.shape[0]
    half = D_MODEL // 2

    def proj(col, lo=0, width=D_MODEL):
        sl = slice(col * D_MODEL + lo, col * D_MODEL + lo + width)
        return _dot(n_ref[...], win_ref[:, sl]) + bin_ref[:, sl]

    @pl.when(pl.program_id(1) == 0)
    def _():
        ext_ref[...] = jnp.zeros(ext_ref.shape, F32)

    n_ref[...] = _rms(h_ref[...], g_ref[...]).astype(BF16)

    a = proj(0) * jax.nn.sigmoid(proj(1))
    for s in range(SUBLANES):
        ext_ref[s, CONV_HALO - s:CONV_HALO - s + tile, :] = a

    def conv_block(r):
        base = r * CONV_ROWS
        groups = CONV_ROWS // SUBLANES
        acc = [cb_ref[...]] * groups
        for k in range(CONV_WIDTH):
            q, s = divmod(CONV_HALO - (CONV_WIDTH - 1) + k, SUBLANES)
            for j in range(groups):
                start = base + (q + j) * SUBLANES
                acc[j] = acc[j] + cw_ref[k] * ext_ref[s, start:start + SUBLANES, :]
        z = _layer_norm(jnp.concatenate(acc, axis=0), clg_ref[...], clb_ref[...])
        za_ref[base:base + CONV_ROWS, :] = (z * jax.nn.sigmoid(z)).astype(BF16)

    def proj_unit(i):
        col, lo = (3, 2, 4, 5)[i // 2], (i % 2) * half
        y = proj(col, lo, half)
        if col == 3:
            t_ref[:, lo:lo + half] = _gelu_tanh(y)
        elif col == 2:
            u_ref[:, lo:lo + half] = _gelu_tanh(y)
        elif col == 4:
            ga_ref[:, lo:lo + half] = jax.nn.sigmoid(y)
        else:
            gb_ref[:, lo:lo + half] = jax.nn.sigmoid(y)

    for r in range(tile // CONV_ROWS):
        conv_block(r)
    for i in range(8):
        proj_unit(i)
    for s in range(SUBLANES):
        ext_ref[s, 0:CONV_HALO, :] = ext_ref[s, tile:tile + CONV_HALO, :]

    v_ref[...] = _layer_norm(t_ref[...], slg_ref[...], slb_ref[...]).astype(BF16)
    row = lax.broadcasted_iota(jnp.int32, (CHUNK, CHUNK), 0)
    col = lax.broadcasted_iota(jnp.int32, (CHUNK, CHUNK), 1)
    w_s = [jnp.where(row >= col, sw_ref[g], 0.0).astype(BF16) for g in range(SGU_GROUPS)]
    for c in range(tile // CHUNK):
        rows = slice(c * CHUNK, (c + 1) * CHUNK)
        for g in range(SGU_GROUPS):
            cols = slice(g * GROUP_DIM, (g + 1) * GROUP_DIM)
            mixed = _dot(w_s[g], v_ref[rows, cols]) + sbias_ref[:, cols]
            gated_ref[rows, cols] = (u_ref[rows, cols] * mixed).astype(BF16)

    merged = (ga_ref[...] * _dot(za_ref[...], wa_ref[...])
              + gb_ref[...] * _dot(gated_ref[...], wb_ref[...]))
    o_ref[...] = h_ref[...] + _dot(merged.astype(BF16), wo_ref[...])


def _tail_kernel(h_ref, xg_ref, wq_ref, kt_ref, v_ref, wo_ref,
                 fg_ref, wg_ref, wu_ref, wd_ref, og_ref, o_ref, attn_ref, act_ref):
    h = h_ref[...]
    q = _dot(_rms(h, xg_ref[...]).astype(BF16), wq_ref[...]).astype(BF16)
    scale = 1.0 / math.sqrt(X_HEAD_DIM)
    for hd in range(X_HEADS):
        cols = slice(hd * X_HEAD_DIM, (hd + 1) * X_HEAD_DIM)
        s = _dot(q[:, cols], kt_ref[cols, :]) * scale
        e = jnp.exp(s - jnp.max(s, axis=-1, keepdims=True))
        p = e / jnp.sum(e, axis=-1, keepdims=True)
        attn_ref[:, cols] = _dot(p.astype(BF16), v_ref[:, cols]).astype(BF16)
    h = h + _dot(attn_ref[...], wo_ref[...])
    h = _swiglu_half_step(h, fg_ref, wg_ref, wu_ref, wd_ref, act_ref)
    o_ref[...] = _rms(h, og_ref[...])


def _resident(shape):
    nd = len(shape)
    return pl.BlockSpec(shape, lambda *_: (0,) * nd, pipeline_mode=pl.Buffered(1))


def _params(semantics):
    return pltpu.CompilerParams(dimension_semantics=semantics, vmem_limit_bytes=VMEM_LIMIT_BYTES)


def _kv_call(mem, mem_norm, w_kv):
    bsz = mem.shape[0]
    return pl.pallas_call(
        _kv_kernel,
        grid=(bsz,),
        in_specs=[pl.BlockSpec((None, N_MEM, D_MODEL), lambda b: (b, 0, 0)),
                  _resident((1, D_MODEL)), _resident((D_MODEL, 2 * D_MODEL))],
        out_specs=[pl.BlockSpec((None, D_MODEL, N_MEM), lambda b: (b, 0, 0)),
                   pl.BlockSpec((None, N_MEM, D_MODEL), lambda b: (b, 0, 0))],
        out_shape=[jax.ShapeDtypeStruct((bsz, D_MODEL, N_MEM), BF16),
                   jax.ShapeDtypeStruct((bsz, N_MEM, D_MODEL), BF16)],
        compiler_params=_params(("arbitrary",)),
        name="kv_call",
    )(mem, mem_norm, w_kv)


def _ffn_call(x2d, norm, wg, wu, wd):
    tokens = x2d.shape[0]
    tok = pl.BlockSpec((FFN_TILE, D_MODEL), lambda i: (i, 0))
    return pl.pallas_call(
        _ffn_kernel,
        grid=(tokens // FFN_TILE,),
        in_specs=[tok, _resident((1, D_MODEL)), _resident((D_MODEL, D_FF)),
                  _resident((D_MODEL, D_FF)), _resident((D_FF, D_MODEL))],
        out_specs=tok,
        out_shape=jax.ShapeDtypeStruct(x2d.shape, F32),
        scratch_shapes=[pltpu.VMEM((FFN_TILE, D_FF), BF16)],
        compiler_params=_params(("arbitrary",)),
        name="ffn_call",
    )(x2d, norm, wg, wu, wd)


def _mixer_call(h, norm, w_in, b_in, conv_w, conv_b, cln_g, cln_b, w_a, sln_g, sln_b,
                sgu_w, sgu_bias, w_b, w_out):
    bsz, seq, _ = h.shape
    tok = pl.BlockSpec((None, MIX_TILE, D_MODEL), lambda b, j: (b, j, 0))
    vec = _resident((1, D_MODEL))
    mat = _resident((D_MODEL, D_MODEL))
    return pl.pallas_call(
        _mixer_kernel,
        grid=(bsz, seq // MIX_TILE),
        in_specs=[tok, vec, _resident((D_MODEL, D_IN)), _resident((1, D_IN)),
                  _resident((CONV_WIDTH, SUBLANES, D_MODEL)), vec, vec, vec, mat,
                  vec, vec, _resident((SGU_GROUPS, CHUNK, CHUNK)), _resident((CHUNK, D_MODEL)),
                  mat, mat],
        out_specs=tok,
        out_shape=jax.ShapeDtypeStruct(h.shape, F32),
        scratch_shapes=[pltpu.VMEM((MIX_TILE, D_MODEL), BF16),
                        pltpu.VMEM((SUBLANES, MIX_TILE + CONV_HALO, D_MODEL), F32),
                        pltpu.VMEM((MIX_TILE, D_MODEL), BF16),
                        pltpu.VMEM((MIX_TILE, D_MODEL), F32),
                        pltpu.VMEM((MIX_TILE, D_MODEL), F32),
                        pltpu.VMEM((MIX_TILE, D_MODEL), F32),
                        pltpu.VMEM((MIX_TILE, D_MODEL), F32),
                        pltpu.VMEM((MIX_TILE, D_MODEL), BF16),
                        pltpu.VMEM((MIX_TILE, D_MODEL), BF16)],
        compiler_params=_params(("arbitrary", "arbitrary")),
        name="mixer_call",
    )(h, norm, w_in, b_in, conv_w, conv_b, cln_g, cln_b, w_a, sln_g, sln_b,
      sgu_w, sgu_bias, w_b, w_out)


def _tail_call(h, xnorm, w_q, kt, v, w_o, fnorm, wg, wu, wd, onorm):
    bsz, seq, _ = h.shape
    tok = pl.BlockSpec((None, FFN_TILE, D_MODEL), lambda b, j: (b, j, 0))
    vec = _resident((1, D_MODEL))
    mat = _resident((D_MODEL, D_MODEL))
    return pl.pallas_call(
        _tail_kernel,
        grid=(bsz, seq // FFN_TILE),
        in_specs=[tok, vec, mat,
                  pl.BlockSpec((None, D_MODEL, N_MEM), lambda b, j: (b, 0, 0)),
                  pl.BlockSpec((None, N_MEM, D_MODEL), lambda b, j: (b, 0, 0)),
                  mat, vec, _resident((D_MODEL, D_FF)), _resident((D_MODEL, D_FF)),
                  _resident((D_FF, D_MODEL)), vec],
        out_specs=tok,
        out_shape=jax.ShapeDtypeStruct(h.shape, F32),
        scratch_shapes=[pltpu.VMEM((FFN_TILE, D_MODEL), BF16),
                        pltpu.VMEM((FFN_TILE, D_FF), BF16)],
        compiler_params=_params(("arbitrary", "arbitrary")),
        name="tail_call",
    )(h, xnorm, w_q, kt, v, w_o, fnorm, wg, wu, wd, onorm)


def kernel(x, mem, ffn1_norm, ffn1_w_gu, ffn1_w_down, mix_norm, w_in, b_in, conv_w, conv_b, conv_ln_g, conv_ln_b, w_a_out, sgu_ln_g, sgu_ln_b, sgu_w, sgu_b, w_b_out, w_out, xattn_norm, mem_norm, w_q, w_kv, w_o, ffn2_norm, ffn2_w_gu, ffn2_w_down, final_norm):
    bsz, seq, _ = x.shape
    depth = ffn1_norm.shape[0]
    assert seq % MIX_TILE == 0 and seq % FFN_TILE == 0 and MIX_TILE % CHUNK == 0

    def vec(p):
        return p.reshape(1, -1)

    def bf(p):
        return p.astype(BF16)

    h = x
    for l in range(depth):
        h = _ffn_call(h.reshape(bsz * seq, D_MODEL), vec(ffn1_norm[l]),
                      bf(ffn1_w_gu[l][:, :D_FF]), bf(ffn1_w_gu[l][:, D_FF:]),
                      bf(ffn1_w_down[l])).reshape(bsz, seq, D_MODEL)
        sgu_bias = jnp.repeat(jnp.transpose(sgu_b[l]), GROUP_DIM, axis=1)
        conv_taps = jnp.broadcast_to(conv_w[l][:, None, :], (CONV_WIDTH, SUBLANES, D_MODEL))
        h = _mixer_call(h, vec(mix_norm[l]), bf(w_in[l]), vec(b_in[l]), conv_taps, vec(conv_b[l]),
                        vec(conv_ln_g[l]), vec(conv_ln_b[l]), bf(w_a_out[l]),
                        vec(sgu_ln_g[l]), vec(sgu_ln_b[l]), sgu_w[l], sgu_bias,
                        bf(w_b_out[l]), bf(w_out[l]))
        kt, v = _kv_call(mem, vec(mem_norm[l]), bf(w_kv[l]))
        last = l == depth - 1
        assert last, "the final norm is fused into the last layer's tail kernel"
        h = _tail_call(h, vec(xattn_norm[l]), bf(w_q[l]), kt, v, bf(w_o[l]),
                       vec(ffn2_norm[l]), bf(ffn2_w_gu[l][:, :D_FF]), bf(ffn2_w_gu[l][:, D_FF:]),
                       bf(ffn2_w_down[l]), vec(final_norm))
    return h
```

```python
import functools
import math

import jax
import jax.numpy as jnp
from jax import lax
from jax.experimental import pallas as pl
from jax.experimental.pallas import tpu as pltpu

D_MODEL = 1024
N_MEM = 256
D_FF = 2816
CONV_WIDTH = 31
SGU_GROUPS = 4
CHUNK = 128
X_HEADS = 4
X_HEAD_DIM = D_MODEL // X_HEADS
GROUP_DIM = D_MODEL // SGU_GROUPS
D_IN = 6 * D_MODEL
EPS_RMS = 1e-6
EPS_LN = 1e-5

F32 = jnp.float32
BF16 = jnp.bfloat16

V7X_VMEM_BYTES = 64 * 1024 * 1024
VMEM_LIMIT_BYTES = V7X_VMEM_BYTES - 8 * 1024 * 1024

SUBLANES = 8
PACK_ROWS = 16
LANES = 128
LANE_TILES = D_MODEL // LANES
assert LANE_TILES == SUBLANES

TOK_TILE = 512
CONV_TILE = 256
FF_CHUNK = 256
NORM_ROWS = 32
PAIRS = TOK_TILE // 2
CONV_PAIRS = CONV_TILE // 2
HALO_PAIRS = 16
CONV_CHAINS = 4
assert PAIRS % CONV_PAIRS == 0 and CONV_PAIRS % HALO_PAIRS == 0 and 2 * (HALO_PAIRS - 1) >= CONV_WIDTH - 1


def _dot(a, b):
    return jnp.dot(a, b, preferred_element_type=F32)


def _rms(x, g):
    ms = jnp.mean(x * x, axis=-1, keepdims=True)
    return x * lax.rsqrt(ms + EPS_RMS) * g


def _layer_norm(x, g, b):
    mu = jnp.mean(x, axis=-1, keepdims=True)
    xc = x - mu
    var = jnp.mean(xc * xc, axis=-1, keepdims=True)
    return xc * lax.rsqrt(var + EPS_LN) * g + b


def _gelu_tanh(x):
    c = math.sqrt(2.0 / math.pi)
    return x * (0.5 * (1.0 + jnp.tanh(c * (x + 0.044715 * (x * x * x)))))


def _swiglu_half_step(x, g_ref, wg_ref, wu_ref, wd_ref, act_ref):
    xn = _rms(x, g_ref[...]).astype(BF16)
    for c in range(D_FF // FF_CHUNK):
        sl = slice(c * FF_CHUNK, (c + 1) * FF_CHUNK)
        gate = _dot(xn, wg_ref[:, sl])
        up = _dot(xn, wu_ref[:, sl])
        act_ref[:, sl] = (gate * jax.nn.sigmoid(gate) * up).astype(BF16)
    return x + 0.5 * _dot(act_ref[...], wd_ref[...])


def _kv_kernel(mem_ref, g_ref, wkv_ref, kt_ref, v_ref):
    memn = _rms(mem_ref[...], g_ref[...]).astype(BF16)
    kv = _dot(memn, wkv_ref[...])
    kt_ref[...] = kv[:, :D_MODEL].T.astype(BF16)
    v_ref[...] = kv[:, D_MODEL:].astype(BF16)


def _ffn_kernel(x_ref, g_ref, wg_ref, wu_ref, wd_ref, o_ref, act_ref):
    o_ref[...] = _swiglu_half_step(x_ref[...], g_ref, wg_ref, wu_ref, wd_ref, act_ref)


def _glu_kernel(h_ref, g_ref, w_ref, b_ref, even_ref, odd_ref, am_ref):
    tile = h_ref.shape[0]
    step = pl.program_id(1)

    @pl.when(step == 0)
    def _():
        am_ref[0:SUBLANES, :] = jnp.zeros((SUBLANES, LANES), F32)
        even_ref[...] = jnp.zeros(even_ref.shape, BF16)
        odd_ref[...] = jnp.zeros(odd_ref.shape, BF16)

    @pl.when(step > 0)
    def _():
        n = _rms(h_ref[...], g_ref[...]).astype(BF16)
        val = _dot(n, w_ref[:, :D_MODEL]) + b_ref[:, :D_MODEL]
        gate = _dot(n, w_ref[:, D_MODEL:]) + b_ref[:, D_MODEL:]
        a = val * jax.nn.sigmoid(gate)
        for g in range(tile // SUBLANES):
            for c in range(LANE_TILES):
                piece = a[g * SUBLANES:(g + 1) * SUBLANES, c * LANES:(c + 1) * LANES]
                start = (g * SUBLANES + 1) * SUBLANES + c
                am_ref[pl.ds(start, SUBLANES, stride=SUBLANES), :] = piece
        for p in range(tile // 2):
            odd_ref[p] = am_ref[PACK_ROWS * p:PACK_ROWS * (p + 1), :].astype(BF16)
            even_ref[p] = am_ref[PACK_ROWS * p + SUBLANES:PACK_ROWS * (p + 1) + SUBLANES, :].astype(BF16)
        am_ref[0:SUBLANES, :] = am_ref[tile * SUBLANES:(tile + 1) * SUBLANES, :]


def _conv_kernel(even_ref, odd_ref, peven_ref, podd_ref, w_ref, cb_ref, lg_ref, lb_ref, o_ref, ym_ref):
    pairs = even_ref.shape[0]

    def operand(first):
        cur, prev, idx = ((even_ref, peven_ref, first // 2) if first % 2 == 0
                          else (odd_ref, podd_ref, (first + 1) // 2))
        return cur[idx] if idx >= 0 else prev[idx + HALO_PAIRS]

    for p0 in range(0, pairs, CONV_CHAINS):
        acc = [jnp.zeros((PACK_ROWS, LANES), F32)] * CONV_CHAINS
        for k in range(CONV_WIDTH):
            for j in range(CONV_CHAINS):
                x = operand(2 * (p0 + j) + k - (CONV_WIDTH - 1))
                acc[j] = acc[j] + w_ref[k].astype(F32) * x.astype(F32)
        for j in range(CONV_CHAINS):
            ym_ref[PACK_ROWS * (p0 + j):PACK_ROWS * (p0 + j + 1), :] = acc[j]
    for r in range(2 * pairs // NORM_ROWS):
        groups = []
        for g in range(NORM_ROWS // SUBLANES):
            t0 = r * NORM_ROWS + g * SUBLANES
            groups.append(jnp.concatenate(
                [ym_ref[pl.ds(t0 * SUBLANES + c, SUBLANES, stride=SUBLANES), :] for c in range(LANE_TILES)], axis=1))
        y = jnp.concatenate(groups, axis=0) + cb_ref[...]
        z = _layer_norm(y, lg_ref[...], lb_ref[...])
        o_ref[r * NORM_ROWS:(r + 1) * NORM_ROWS, :] = (z * jax.nn.sigmoid(z)).astype(BF16)


def _mix_kernel(h_ref, za_ref, g_ref, win_ref, bin_ref, wa_ref, slg_ref, slb_ref, sw_ref, sbias_ref,
                wb_ref, wo_ref, o_ref, n_ref, v_ref, gated_ref):
    tile = h_ref.shape[0]
    n_ref[...] = _rms(h_ref[...], g_ref[...]).astype(BF16)

    def proj(col):
        sl = slice(col * D_MODEL, (col + 1) * D_MODEL)
        return _dot(n_ref[...], win_ref[:, sl]) + bin_ref[:, sl]

    v_ref[...] = _layer_norm(_gelu_tanh(proj(1)), slg_ref[...], slb_ref[...]).astype(BF16)
    row = lax.broadcasted_iota(jnp.int32, (CHUNK, CHUNK), 0)
    col = lax.broadcasted_iota(jnp.int32, (CHUNK, CHUNK), 1)
    w_s = [jnp.where(row >= col, sw_ref[g], 0.0).astype(BF16) for g in range(SGU_GROUPS)]
    u = _gelu_tanh(proj(0))
    for c in range(tile // CHUNK):
        rows = slice(c * CHUNK, (c + 1) * CHUNK)
        for g in range(SGU_GROUPS):
            cols = slice(g * GROUP_DIM, (g + 1) * GROUP_DIM)
            mixed = _dot(w_s[g], v_ref[rows, cols]) + sbias_ref[:, cols]
            gated_ref[rows, cols] = (u[rows, cols] * mixed).astype(BF16)
    merged = (jax.nn.sigmoid(proj(2)) * _dot(za_ref[...], wa_ref[...])
              + jax.nn.sigmoid(proj(3)) * _dot(gated_ref[...], wb_ref[...]))
    o_ref[...] = h_ref[...] + _dot(merged.astype(BF16), wo_ref[...])


def _tail_kernel(h_ref, xg_ref, wq_ref, kt_ref, v_ref, wo_ref,
                 fg_ref, wg_ref, wu_ref, wd_ref, og_ref, o_ref, attn_ref, act_ref):
    h = h_ref[...]
    q = _dot(_rms(h, xg_ref[...]).astype(BF16), wq_ref[...]).astype(BF16)
    scale = 1.0 / math.sqrt(X_HEAD_DIM)
    for hd in range(X_HEADS):
        cols = slice(hd * X_HEAD_DIM, (hd + 1) * X_HEAD_DIM)
        s = _dot(q[:, cols], kt_ref[cols, :]) * scale
        e = jnp.exp(s - jnp.max(s, axis=-1, keepdims=True))
        p = e / jnp.sum(e, axis=-1, keepdims=True)
        attn_ref[:, cols] = _dot(p.astype(BF16), v_ref[:, cols]).astype(BF16)
    h = h + _dot(attn_ref[...], wo_ref[...])
    h = _swiglu_half_step(h, fg_ref, wg_ref, wu_ref, wd_ref, act_ref)
    o_ref[...] = _rms(h, og_ref[...])


def _resident(shape):
    nd = len(shape)
    return pl.BlockSpec(shape, lambda *_: (0,) * nd, pipeline_mode=pl.Buffered(1))


def _params(semantics):
    return pltpu.CompilerParams(dimension_semantics=semantics, vmem_limit_bytes=VMEM_LIMIT_BYTES)


def _kv_call(mem, mem_norm, w_kv):
    bsz = mem.shape[0]
    return pl.pallas_call(
        _kv_kernel,
        grid=(bsz,),
        in_specs=[pl.BlockSpec((None, N_MEM, D_MODEL), lambda b: (b, 0, 0)),
                  _resident((1, D_MODEL)), _resident((D_MODEL, 2 * D_MODEL))],
        out_specs=[pl.BlockSpec((None, D_MODEL, N_MEM), lambda b: (b, 0, 0)),
                   pl.BlockSpec((None, N_MEM, D_MODEL), lambda b: (b, 0, 0))],
        out_shape=[jax.ShapeDtypeStruct((bsz, D_MODEL, N_MEM), BF16),
                   jax.ShapeDtypeStruct((bsz, N_MEM, D_MODEL), BF16)],
        compiler_params=_params(("arbitrary",)),
        name="kv_call",
    )(mem, mem_norm, w_kv)


def _ffn_call(x2d, norm, wg, wu, wd):
    tokens = x2d.shape[0]
    tok = pl.BlockSpec((TOK_TILE, D_MODEL), lambda i: (i, 0))
    return pl.pallas_call(
        _ffn_kernel,
        grid=(tokens // TOK_TILE,),
        in_specs=[tok, _resident((1, D_MODEL)), _resident((D_MODEL, D_FF)),
                  _resident((D_MODEL, D_FF)), _resident((D_FF, D_MODEL))],
        out_specs=tok,
        out_shape=jax.ShapeDtypeStruct(x2d.shape, F32),
        scratch_shapes=[pltpu.VMEM((TOK_TILE, D_FF), BF16)],
        compiler_params=_params(("arbitrary",)),
        name="ffn_call",
    )(x2d, norm, wg, wu, wd)


def _glu_call(h, norm, w_glu, b_glu):
    bsz, seq, _ = h.shape
    tiles = seq // TOK_TILE
    pair_block = pl.BlockSpec((None, PAIRS, PACK_ROWS, LANES), lambda b, j: (b, j, 0, 0))
    pair_shape = jax.ShapeDtypeStruct((bsz, (tiles + 1) * PAIRS, PACK_ROWS, LANES), BF16)
    return pl.pallas_call(
        _glu_kernel,
        grid=(bsz, tiles + 1),
        in_specs=[pl.BlockSpec((None, TOK_TILE, D_MODEL), lambda b, j: (b, jnp.maximum(j - 1, 0), 0)),
                  _resident((1, D_MODEL)), _resident((D_MODEL, 2 * D_MODEL)), _resident((1, 2 * D_MODEL))],
        out_specs=[pair_block, pair_block],
        out_shape=[pair_shape, pair_shape],
        scratch_shapes=[pltpu.VMEM(((TOK_TILE + 1) * SUBLANES, LANES), F32)],
        compiler_params=_params(("arbitrary", "arbitrary")),
        name="glu_call",
    )(h, norm, w_glu, b_glu)


def _conv_call(even, odd, seq, taps, conv_b, ln_g, ln_b):
    bsz = even.shape[0]
    lead = PAIRS // CONV_PAIRS
    halo_per_block = CONV_PAIRS // HALO_PAIRS
    cur = pl.BlockSpec((None, CONV_PAIRS, PACK_ROWS, LANES), lambda b, i: (b, i + lead, 0, 0))
    prev = pl.BlockSpec((None, HALO_PAIRS, PACK_ROWS, LANES),
                        lambda b, i: (b, (i + lead) * halo_per_block - 1, 0, 0))
    vec = _resident((1, D_MODEL))
    return pl.pallas_call(
        _conv_kernel,
        grid=(bsz, seq // CONV_TILE),
        in_specs=[cur, cur, prev, prev, _resident((CONV_WIDTH, PACK_ROWS, LANES)), vec, vec, vec],
        out_specs=pl.BlockSpec((None, CONV_TILE, D_MODEL), lambda b, i: (b, i, 0)),
        out_shape=jax.ShapeDtypeStruct((bsz, seq, D_MODEL), BF16),
        scratch_shapes=[pltpu.VMEM((CONV_TILE * SUBLANES, LANES), F32)],
        compiler_params=_params(("arbitrary", "arbitrary")),
        name="conv_call",
    )(even, odd, even, odd, taps, conv_b, ln_g, ln_b)


def _mix_call(h, za, norm, w_mix, b_mix, w_a, sln_g, sln_b, sgu_w, sgu_bias, w_b, w_out):
    bsz, seq, _ = h.shape
    tok = pl.BlockSpec((None, TOK_TILE, D_MODEL), lambda b, j: (b, j, 0))
    vec = _resident((1, D_MODEL))
    mat = _resident((D_MODEL, D_MODEL))
    return pl.pallas_call(
        _mix_kernel,
        grid=(bsz, seq // TOK_TILE),
        in_specs=[tok, tok, vec, _resident((D_MODEL, 4 * D_MODEL)), _resident((1, 4 * D_MODEL)), mat,
                  vec, vec, _resident((SGU_GROUPS, CHUNK, CHUNK)), _resident((CHUNK, D_MODEL)), mat, mat],
        out_specs=tok,
        out_shape=jax.ShapeDtypeStruct(h.shape, F32),
        scratch_shapes=[pltpu.VMEM((TOK_TILE, D_MODEL), BF16),
                        pltpu.VMEM((TOK_TILE, D_MODEL), BF16),
                        pltpu.VMEM((TOK_TILE, D_MODEL), BF16)],
        compiler_params=_params(("arbitrary", "arbitrary")),
        name="mix_call",
    )(h, za, norm, w_mix, b_mix, w_a, sln_g, sln_b, sgu_w, sgu_bias, w_b, w_out)


def _tail_call(h, xnorm, w_q, kt, v, w_o, fnorm, wg, wu, wd, onorm):
    bsz, seq, _ = h.shape
    tok = pl.BlockSpec((None, TOK_TILE, D_MODEL), lambda b, j: (b, j, 0))
    vec = _resident((1, D_MODEL))
    mat = _resident((D_MODEL, D_MODEL))
    return pl.pallas_call(
        _tail_kernel,
        grid=(bsz, seq // TOK_TILE),
        in_specs=[tok, vec, mat,
                  pl.BlockSpec((None, D_MODEL, N_MEM), lambda b, j: (b, 0, 0)),
                  pl.BlockSpec((None, N_MEM, D_MODEL), lambda b, j: (b, 0, 0)),
                  mat, vec, _resident((D_MODEL, D_FF)), _resident((D_MODEL, D_FF)),
                  _resident((D_FF, D_MODEL)), vec],
        out_specs=tok,
        out_shape=jax.ShapeDtypeStruct(h.shape, F32),
        scratch_shapes=[pltpu.VMEM((TOK_TILE, D_MODEL), BF16),
                        pltpu.VMEM((TOK_TILE, D_FF), BF16)],
        compiler_params=_params(("arbitrary", "arbitrary")),
        name="tail_call",
    )(h, xnorm, w_q, kt, v, w_o, fnorm, wg, wu, wd, onorm)


def kernel(x, mem, ffn1_norm, ffn1_w_gu, ffn1_w_down, mix_norm, w_in, b_in, conv_w, conv_b, conv_ln_g, conv_ln_b, w_a_out, sgu_ln_g, sgu_ln_b, sgu_w, sgu_b, w_b_out, w_out, xattn_norm, mem_norm, w_q, w_kv, w_o, ffn2_norm, ffn2_w_gu, ffn2_w_down, final_norm):
    bsz, seq, _ = x.shape
    depth = ffn1_norm.shape[0]
    assert seq % TOK_TILE == 0 and seq % CONV_TILE == 0 and TOK_TILE % CHUNK == 0

    def vec(p):
        return p.reshape(1, -1)

    def bf(p):
        return p.astype(BF16)

    h = x
    for l in range(depth):
        h = _ffn_call(h.reshape(bsz * seq, D_MODEL), vec(ffn1_norm[l]),
                      bf(ffn1_w_gu[l][:, :D_FF]), bf(ffn1_w_gu[l][:, D_FF:]),
                      bf(ffn1_w_down[l])).reshape(bsz, seq, D_MODEL)
        even, odd = _glu_call(h, vec(mix_norm[l]), bf(w_in[l][:, :2 * D_MODEL]), vec(b_in[l][:2 * D_MODEL]))
        taps = jnp.tile(bf(conv_w[l]).reshape(CONV_WIDTH, LANE_TILES, LANES), (1, 2, 1))
        za = _conv_call(even, odd, seq, taps, vec(conv_b[l]), vec(conv_ln_g[l]), vec(conv_ln_b[l]))
        sgu_bias = jnp.repeat(jnp.transpose(sgu_b[l]), GROUP_DIM, axis=1)
        h = _mix_call(h, za, vec(mix_norm[l]), bf(w_in[l][:, 2 * D_MODEL:]), vec(b_in[l][2 * D_MODEL:]),
                      bf(w_a_out[l]), vec(sgu_ln_g[l]), vec(sgu_ln_b[l]), sgu_w[l], sgu_bias,
                      bf(w_b_out[l]), bf(w_out[l]))
        kt, v = _kv_call(mem, vec(mem_norm[l]), bf(w_kv[l]))
        last = l == depth - 1
        assert last, "the final norm is fused into the last layer's tail kernel"
        h = _tail_call(h, vec(xattn_norm[l]), bf(w_q[l]), kt, v, bf(w_o[l]),
                       vec(ffn2_norm[l]), bf(ffn2_w_gu[l][:, :D_FF]), bf(ffn2_w_gu[l][:, D_FF:]),
                       bf(ffn2_w_down[l]), vec(final_norm))
    return h
```

```python
import functools
import math

import jax
import jax.numpy as jnp
from jax import lax
from jax.experimental import pallas as pl
from jax.experimental.pallas import tpu as pltpu

D_MODEL = 1024
N_MEM = 256
D_FF = 2816
CONV_WIDTH = 31
SGU_GROUPS = 4
CHUNK = 128
X_HEADS = 4
X_HEAD_DIM = D_MODEL // X_HEADS
GROUP_DIM = D_MODEL // SGU_GROUPS
D_IN = 6 * D_MODEL
EPS_RMS = 1e-6
EPS_LN = 1e-5

F32 = jnp.float32
BF16 = jnp.bfloat16

V7X_VMEM_BYTES = 64 * 1024 * 1024
VMEM_LIMIT_BYTES = V7X_VMEM_BYTES - 8 * 1024 * 1024

SUBLANES = 8
PACK_ROWS = 16
LANES = 128
LANE_TILES = D_MODEL // LANES
assert LANE_TILES == SUBLANES

TOK_TILE = 512
FF_CHUNK = 256
NORM_ROWS = 32
PAIRS = TOK_TILE // 2
HALO_PAIRS = 16
CONV_CHAINS = 4
assert PAIRS % HALO_PAIRS == 0 and PAIRS % CONV_CHAINS == 0 and 2 * (HALO_PAIRS - 1) >= CONV_WIDTH - 1


def _dot(a, b):
    return jnp.dot(a, b, preferred_element_type=F32)


def _rms(x, g):
    ms = jnp.mean(x * x, axis=-1, keepdims=True)
    return x * lax.rsqrt(ms + EPS_RMS) * g


def _layer_norm(x, g, b):
    mu = jnp.mean(x, axis=-1, keepdims=True)
    xc = x - mu
    var = jnp.mean(xc * xc, axis=-1, keepdims=True)
    return xc * lax.rsqrt(var + EPS_LN) * g + b


def _gelu_tanh(x):
    c = math.sqrt(2.0 / math.pi)
    return x * (0.5 * (1.0 + jnp.tanh(c * (x + 0.044715 * (x * x * x)))))


def _swiglu_half_step(x, g_ref, wg_ref, wu_ref, wd_ref, act_ref):
    xn = _rms(x, g_ref[...]).astype(BF16)
    for c in range(D_FF // FF_CHUNK):
        sl = slice(c * FF_CHUNK, (c + 1) * FF_CHUNK)
        gate = _dot(xn, wg_ref[:, sl])
        up = _dot(xn, wu_ref[:, sl])
        act_ref[:, sl] = (gate * jax.nn.sigmoid(gate) * up).astype(BF16)
    return x + 0.5 * _dot(act_ref[...], wd_ref[...])


def _kv_kernel(mem_ref, g_ref, wkv_ref, kt_ref, v_ref):
    memn = _rms(mem_ref[...], g_ref[...]).astype(BF16)
    kv = _dot(memn, wkv_ref[...])
    kt_ref[...] = kv[:, :D_MODEL].T.astype(BF16)
    v_ref[...] = kv[:, D_MODEL:].astype(BF16)


def _ffn_kernel(x_ref, g_ref, wg_ref, wu_ref, wd_ref, o_ref, act_ref):
    o_ref[...] = _swiglu_half_step(x_ref[...], g_ref, wg_ref, wu_ref, wd_ref, act_ref)


def _glu_kernel(h_ref, g_ref, w_ref, b_ref, even_ref, odd_ref, am_ref):
    tile = h_ref.shape[0]
    step = pl.program_id(1)

    @pl.when(step == 0)
    def _():
        am_ref[0:SUBLANES, :] = jnp.zeros((SUBLANES, LANES), F32)
        even_ref[...] = jnp.zeros(even_ref.shape, BF16)
        odd_ref[...] = jnp.zeros(odd_ref.shape, BF16)

    @pl.when(step > 0)
    def _():
        n = _rms(h_ref[...], g_ref[...]).astype(BF16)
        val = _dot(n, w_ref[:, :D_MODEL]) + b_ref[:, :D_MODEL]
        gate = _dot(n, w_ref[:, D_MODEL:]) + b_ref[:, D_MODEL:]
        a = val * jax.nn.sigmoid(gate)
        for g in range(tile // SUBLANES):
            for c in range(LANE_TILES):
                piece = a[g * SUBLANES:(g + 1) * SUBLANES, c * LANES:(c + 1) * LANES]
                start = (g * SUBLANES + 1) * SUBLANES + c
                am_ref[pl.ds(start, SUBLANES, stride=SUBLANES), :] = piece
        for p in range(tile // 2):
            odd_ref[p] = am_ref[PACK_ROWS * p:PACK_ROWS * (p + 1), :].astype(BF16)
            even_ref[p] = am_ref[PACK_ROWS * p + SUBLANES:PACK_ROWS * (p + 1) + SUBLANES, :].astype(BF16)
        am_ref[0:SUBLANES, :] = am_ref[tile * SUBLANES:(tile + 1) * SUBLANES, :]


def _conv_norm_swish(even_ref, odd_ref, peven_ref, podd_ref, w_ref, cb_ref, lg_ref, lb_ref, o_ref, ym_ref):
    pairs = even_ref.shape[0]

    def operand(first):
        cur, prev, idx = ((even_ref, peven_ref, first // 2) if first % 2 == 0
                          else (odd_ref, podd_ref, (first + 1) // 2))
        return cur[idx] if idx >= 0 else prev[idx + HALO_PAIRS]

    for p0 in range(0, pairs, CONV_CHAINS):
        acc = [jnp.zeros((PACK_ROWS, LANES), F32)] * CONV_CHAINS
        for k in range(CONV_WIDTH):
            for j in range(CONV_CHAINS):
                x = operand(2 * (p0 + j) + k - (CONV_WIDTH - 1))
                acc[j] = acc[j] + w_ref[k].astype(F32) * x.astype(F32)
        for j in range(CONV_CHAINS):
            ym_ref[PACK_ROWS * (p0 + j):PACK_ROWS * (p0 + j + 1), :] = acc[j]
    for r in range(2 * pairs // NORM_ROWS):
        groups = []
        for g in range(NORM_ROWS // SUBLANES):
            t0 = r * NORM_ROWS + g * SUBLANES
            groups.append(jnp.concatenate(
                [ym_ref[pl.ds(t0 * SUBLANES + c, SUBLANES, stride=SUBLANES), :] for c in range(LANE_TILES)], axis=1))
        y = jnp.concatenate(groups, axis=0) + cb_ref[...]
        z = _layer_norm(y, lg_ref[...], lb_ref[...])
        o_ref[r * NORM_ROWS:(r + 1) * NORM_ROWS, :] = (z * jax.nn.sigmoid(z)).astype(BF16)


def _mix_kernel(h_ref, even_ref, odd_ref, peven_ref, podd_ref, cw_ref, cb_ref, clg_ref, clb_ref,
                g_ref, win_ref, bin_ref, wa_ref, slg_ref, slb_ref, sw_ref, sbias_ref,
                wb_ref, wo_ref, o_ref, ym_ref, za_ref, n_ref, v_ref, gated_ref):
    tile = h_ref.shape[0]
    _conv_norm_swish(even_ref, odd_ref, peven_ref, podd_ref, cw_ref, cb_ref, clg_ref, clb_ref, za_ref, ym_ref)
    n_ref[...] = _rms(h_ref[...], g_ref[...]).astype(BF16)

    def proj(col):
        sl = slice(col * D_MODEL, (col + 1) * D_MODEL)
        return _dot(n_ref[...], win_ref[:, sl]) + bin_ref[:, sl]

    v_ref[...] = _layer_norm(_gelu_tanh(proj(1)), slg_ref[...], slb_ref[...]).astype(BF16)
    row = lax.broadcasted_iota(jnp.int32, (CHUNK, CHUNK), 0)
    col = lax.broadcasted_iota(jnp.int32, (CHUNK, CHUNK), 1)
    w_s = [jnp.where(row >= col, sw_ref[g], 0.0).astype(BF16) for g in range(SGU_GROUPS)]
    u = _gelu_tanh(proj(0))
    for c in range(tile // CHUNK):
        rows = slice(c * CHUNK, (c + 1) * CHUNK)
        for g in range(SGU_GROUPS):
            cols = slice(g * GROUP_DIM, (g + 1) * GROUP_DIM)
            mixed = _dot(w_s[g], v_ref[rows, cols]) + sbias_ref[:, cols]
            gated_ref[rows, cols] = (u[rows, cols] * mixed).astype(BF16)
    merged = (jax.nn.sigmoid(proj(2)) * _dot(za_ref[...], wa_ref[...])
              + jax.nn.sigmoid(proj(3)) * _dot(gated_ref[...], wb_ref[...]))
    o_ref[...] = h_ref[...] + _dot(merged.astype(BF16), wo_ref[...])


def _tail_kernel(h_ref, xg_ref, wq_ref, kt_ref, v_ref, wo_ref,
                 fg_ref, wg_ref, wu_ref, wd_ref, og_ref, o_ref, attn_ref, act_ref):
    h = h_ref[...]
    q = _dot(_rms(h, xg_ref[...]).astype(BF16), wq_ref[...]).astype(BF16)
    scale = 1.0 / math.sqrt(X_HEAD_DIM)
    for hd in range(X_HEADS):
        cols = slice(hd * X_HEAD_DIM, (hd + 1) * X_HEAD_DIM)
        s = _dot(q[:, cols], kt_ref[cols, :]) * scale
        e = jnp.exp(s - jnp.max(s, axis=-1, keepdims=True))
        p = e / jnp.sum(e, axis=-1, keepdims=True)
        attn_ref[:, cols] = _dot(p.astype(BF16), v_ref[:, cols]).astype(BF16)
    h = h + _dot(attn_ref[...], wo_ref[...])
    h = _swiglu_half_step(h, fg_ref, wg_ref, wu_ref, wd_ref, act_ref)
    o_ref[...] = _rms(h, og_ref[...])


def _resident(shape):
    nd = len(shape)
    return pl.BlockSpec(shape, lambda *_: (0,) * nd, pipeline_mode=pl.Buffered(1))


def _params(semantics):
    return pltpu.CompilerParams(dimension_semantics=semantics, vmem_limit_bytes=VMEM_LIMIT_BYTES)


def _kv_call(mem, mem_norm, w_kv):
    bsz = mem.shape[0]
    return pl.pallas_call(
        _kv_kernel,
        grid=(bsz,),
        in_specs=[pl.BlockSpec((None, N_MEM, D_MODEL), lambda b: (b, 0, 0)),
                  _resident((1, D_MODEL)), _resident((D_MODEL, 2 * D_MODEL))],
        out_specs=[pl.BlockSpec((None, D_MODEL, N_MEM), lambda b: (b, 0, 0)),
                   pl.BlockSpec((None, N_MEM, D_MODEL), lambda b: (b, 0, 0))],
        out_shape=[jax.ShapeDtypeStruct((bsz, D_MODEL, N_MEM), BF16),
                   jax.ShapeDtypeStruct((bsz, N_MEM, D_MODEL), BF16)],
        compiler_params=_params(("arbitrary",)),
        name="kv_call",
    )(mem, mem_norm, w_kv)


def _ffn_call(x2d, norm, wg, wu, wd):
    tokens = x2d.shape[0]
    tok = pl.BlockSpec((TOK_TILE, D_MODEL), lambda i: (i, 0))
    return pl.pallas_call(
        _ffn_kernel,
        grid=(tokens // TOK_TILE,),
        in_specs=[tok, _resident((1, D_MODEL)), _resident((D_MODEL, D_FF)),
                  _resident((D_MODEL, D_FF)), _resident((D_FF, D_MODEL))],
        out_specs=tok,
        out_shape=jax.ShapeDtypeStruct(x2d.shape, F32),
        scratch_shapes=[pltpu.VMEM((TOK_TILE, D_FF), BF16)],
        compiler_params=_params(("arbitrary",)),
        name="ffn_call",
    )(x2d, norm, wg, wu, wd)


def _glu_call(h, norm, w_glu, b_glu):
    bsz, seq, _ = h.shape
    tiles = seq // TOK_TILE
    pair_block = pl.BlockSpec((None, PAIRS, PACK_ROWS, LANES), lambda b, j: (b, j, 0, 0))
    pair_shape = jax.ShapeDtypeStruct((bsz, (tiles + 1) * PAIRS, PACK_ROWS, LANES), BF16)
    return pl.pallas_call(
        _glu_kernel,
        grid=(bsz, tiles + 1),
        in_specs=[pl.BlockSpec((None, TOK_TILE, D_MODEL), lambda b, j: (b, jnp.maximum(j - 1, 0), 0)),
                  _resident((1, D_MODEL)), _resident((D_MODEL, 2 * D_MODEL)), _resident((1, 2 * D_MODEL))],
        out_specs=[pair_block, pair_block],
        out_shape=[pair_shape, pair_shape],
        scratch_shapes=[pltpu.VMEM(((TOK_TILE + 1) * SUBLANES, LANES), F32)],
        compiler_params=_params(("arbitrary", "arbitrary")),
        name="glu_call",
    )(h, norm, w_glu, b_glu)


def _mix_call(h, even, odd, taps, conv_b, cln_g, cln_b, norm, w_mix, b_mix, w_a, sln_g, sln_b,
              sgu_w, sgu_bias, w_b, w_out):
    bsz, seq, _ = h.shape
    halo_per_block = PAIRS // HALO_PAIRS
    tok = pl.BlockSpec((None, TOK_TILE, D_MODEL), lambda b, j: (b, j, 0))
    cur = pl.BlockSpec((None, PAIRS, PACK_ROWS, LANES), lambda b, j: (b, j + 1, 0, 0))
    prev = pl.BlockSpec((None, HALO_PAIRS, PACK_ROWS, LANES),
                        lambda b, j: (b, (j + 1) * halo_per_block - 1, 0, 0))
    vec = _resident((1, D_MODEL))
    mat = _resident((D_MODEL, D_MODEL))
    return pl.pallas_call(
        _mix_kernel,
        grid=(bsz, seq // TOK_TILE),
        in_specs=[tok, cur, cur, prev, prev, _resident((CONV_WIDTH, PACK_ROWS, LANES)), vec, vec, vec,
                  vec, _resident((D_MODEL, 4 * D_MODEL)), _resident((1, 4 * D_MODEL)), mat,
                  vec, vec, _resident((SGU_GROUPS, CHUNK, CHUNK)), _resident((CHUNK, D_MODEL)), mat, mat],
        out_specs=tok,
        out_shape=jax.ShapeDtypeStruct(h.shape, F32),
        scratch_shapes=[pltpu.VMEM((TOK_TILE * SUBLANES, LANES), F32),
                        pltpu.VMEM((TOK_TILE, D_MODEL), BF16),
                        pltpu.VMEM((TOK_TILE, D_MODEL), BF16),
                        pltpu.VMEM((TOK_TILE, D_MODEL), BF16),
                        pltpu.VMEM((TOK_TILE, D_MODEL), BF16)],
        compiler_params=_params(("arbitrary", "arbitrary")),
        name="mix_call",
    )(h, even, odd, even, odd, taps, conv_b, cln_g, cln_b, norm, w_mix, b_mix, w_a, sln_g, sln_b,
      sgu_w, sgu_bias, w_b, w_out)


def _tail_call(h, xnorm, w_q, kt, v, w_o, fnorm, wg, wu, wd, onorm):
    bsz, seq, _ = h.shape
    tok = pl.BlockSpec((None, TOK_TILE, D_MODEL), lambda b, j: (b, j, 0))
    vec = _resident((1, D_MODEL))
    mat = _resident((D_MODEL, D_MODEL))
    return pl.pallas_call(
        _tail_kernel,
        grid=(bsz, seq // TOK_TILE),
        in_specs=[tok, vec, mat,
                  pl.BlockSpec((None, D_MODEL, N_MEM), lambda b, j: (b, 0, 0)),
                  pl.BlockSpec((None, N_MEM, D_MODEL), lambda b, j: (b, 0, 0)),
                  mat, vec, _resident((D_MODEL, D_FF)), _resident((D_MODEL, D_FF)),
                  _resident((D_FF, D_MODEL)), vec],
        out_specs=tok,
        out_shape=jax.ShapeDtypeStruct(h.shape, F32),
        scratch_shapes=[pltpu.VMEM((TOK_TILE, D_MODEL), BF16),
                        pltpu.VMEM((TOK_TILE, D_FF), BF16)],
        compiler_params=_params(("arbitrary", "arbitrary")),
        name="tail_call",
    )(h, xnorm, w_q, kt, v, w_o, fnorm, wg, wu, wd, onorm)


def kernel(x, mem, ffn1_norm, ffn1_w_gu, ffn1_w_down, mix_norm, w_in, b_in, conv_w, conv_b, conv_ln_g, conv_ln_b, w_a_out, sgu_ln_g, sgu_ln_b, sgu_w, sgu_b, w_b_out, w_out, xattn_norm, mem_norm, w_q, w_kv, w_o, ffn2_norm, ffn2_w_gu, ffn2_w_down, final_norm):
    bsz, seq, _ = x.shape
    depth = ffn1_norm.shape[0]
    assert seq % TOK_TILE == 0 and TOK_TILE % CHUNK == 0

    def vec(p):
        return p.reshape(1, -1)

    def bf(p):
        return p.astype(BF16)

    h = x
    for l in range(depth):
        h = _ffn_call(h.reshape(bsz * seq, D_MODEL), vec(ffn1_norm[l]),
                      bf(ffn1_w_gu[l][:, :D_FF]), bf(ffn1_w_gu[l][:, D_FF:]),
                      bf(ffn1_w_down[l])).reshape(bsz, seq, D_MODEL)
        even, odd = _glu_call(h, vec(mix_norm[l]), bf(w_in[l][:, :2 * D_MODEL]), vec(b_in[l][:2 * D_MODEL]))
        taps = jnp.tile(bf(conv_w[l]).reshape(CONV_WIDTH, LANE_TILES, LANES), (1, 2, 1))
        sgu_bias = jnp.repeat(jnp.transpose(sgu_b[l]), GROUP_DIM, axis=1)
        h = _mix_call(h, even, odd, taps, vec(conv_b[l]), vec(conv_ln_g[l]), vec(conv_ln_b[l]),
                      vec(mix_norm[l]), bf(w_in[l][:, 2 * D_MODEL:]), vec(b_in[l][2 * D_MODEL:]),
                      bf(w_a_out[l]), vec(sgu_ln_g[l]), vec(sgu_ln_b[l]), sgu_w[l], sgu_bias,
                      bf(w_b_out[l]), bf(w_out[l]))
        kt, v = _kv_call(mem, vec(mem_norm[l]), bf(w_kv[l]))
        last = l == depth - 1
        assert last, "the final norm is fused into the last layer's tail kernel"
        h = _tail_call(h, vec(xattn_norm[l]), bf(w_q[l]), kt, v, bf(w_o[l]),
                       vec(ffn2_norm[l]), bf(ffn2_w_gu[l][:, :D_FF]), bf(ffn2_w_gu[l][:, D_FF:]),
                       bf(ffn2_w_down[l]), vec(final_norm))
    return h
```

```python
import functools
import math

import jax
import jax.numpy as jnp
from jax import lax
from jax.experimental import pallas as pl
from jax.experimental.pallas import tpu as pltpu

D_MODEL = 1024
N_MEM = 256
D_FF = 2816
CONV_WIDTH = 31
SGU_GROUPS = 4
CHUNK = 128
X_HEADS = 4
X_HEAD_DIM = D_MODEL // X_HEADS
GROUP_DIM = D_MODEL // SGU_GROUPS
D_IN = 6 * D_MODEL
EPS_RMS = 1e-6
EPS_LN = 1e-5

F32 = jnp.float32
BF16 = jnp.bfloat16

V7X_VMEM_BYTES = 64 * 1024 * 1024
VMEM_LIMIT_BYTES = V7X_VMEM_BYTES - 8 * 1024 * 1024

SUBLANES = 8
PACK_ROWS = 16
LANES = 128
LANE_TILES = D_MODEL // LANES
assert LANE_TILES == SUBLANES

TOK_TILE = 512
FF_CHUNK = 256
NORM_ROWS = 32
PAIRS = TOK_TILE // 2
HALO_PAIRS = 16
CONV_CHAINS = 4
assert PAIRS % HALO_PAIRS == 0 and PAIRS % CONV_CHAINS == 0 and 2 * (HALO_PAIRS - 1) >= CONV_WIDTH - 1


def _dot(a, b):
    return jnp.dot(a, b, preferred_element_type=F32)


def _rms(x, g):
    ms = jnp.mean(x * x, axis=-1, keepdims=True)
    return x * lax.rsqrt(ms + EPS_RMS) * g


def _layer_norm(x, g, b):
    mu = jnp.mean(x, axis=-1, keepdims=True)
    xc = x - mu
    var = jnp.mean(xc * xc, axis=-1, keepdims=True)
    return xc * lax.rsqrt(var + EPS_LN) * g + b


def _gelu_tanh(x):
    c = math.sqrt(2.0 / math.pi)
    return x * (0.5 * (1.0 + jnp.tanh(c * (x + 0.044715 * (x * x * x)))))


def _swiglu_half_step(x, g_ref, wg_ref, wu_ref, wd_ref, act_ref):
    xn = _rms(x, g_ref[...]).astype(BF16)
    for c in range(D_FF // FF_CHUNK):
        sl = slice(c * FF_CHUNK, (c + 1) * FF_CHUNK)
        gate = _dot(xn, wg_ref[:, sl])
        up = _dot(xn, wu_ref[:, sl])
        act_ref[:, sl] = (gate * jax.nn.sigmoid(gate) * up).astype(BF16)
    return x + 0.5 * _dot(act_ref[...], wd_ref[...])


def _kv_kernel(mem_ref, g_ref, wkv_ref, kt_ref, v_ref):
    memn = _rms(mem_ref[...], g_ref[...]).astype(BF16)
    kv = _dot(memn, wkv_ref[...])
    kt_ref[...] = kv[:, :D_MODEL].T.astype(BF16)
    v_ref[...] = kv[:, D_MODEL:].astype(BF16)


def _ffn_glu_kernel(x_ref, fg_ref, wg_ref, wu_ref, wd_ref, mg_ref, w_ref, b_ref,
                    h_ref, even_ref, odd_ref, act_ref, am_ref):
    tile = x_ref.shape[0]
    step = pl.program_id(1)

    @pl.when(step == 0)
    def _():
        am_ref[0:SUBLANES, :] = jnp.zeros((SUBLANES, LANES), F32)
        even_ref[...] = jnp.zeros(even_ref.shape, BF16)
        odd_ref[...] = jnp.zeros(odd_ref.shape, BF16)
        h_ref[...] = jnp.zeros(h_ref.shape, F32)

    @pl.when(step > 0)
    def _():
        h = _swiglu_half_step(x_ref[...], fg_ref, wg_ref, wu_ref, wd_ref, act_ref)
        h_ref[...] = h
        n = _rms(h, mg_ref[...]).astype(BF16)
        val = _dot(n, w_ref[:, :D_MODEL]) + b_ref[:, :D_MODEL]
        gate = _dot(n, w_ref[:, D_MODEL:]) + b_ref[:, D_MODEL:]
        a = val * jax.nn.sigmoid(gate)
        for g in range(tile // SUBLANES):
            for c in range(LANE_TILES):
                piece = a[g * SUBLANES:(g + 1) * SUBLANES, c * LANES:(c + 1) * LANES]
                start = (g * SUBLANES + 1) * SUBLANES + c
                am_ref[pl.ds(start, SUBLANES, stride=SUBLANES), :] = piece
        for p in range(tile // 2):
            odd_ref[p] = am_ref[PACK_ROWS * p:PACK_ROWS * (p + 1), :].astype(BF16)
            even_ref[p] = am_ref[PACK_ROWS * p + SUBLANES:PACK_ROWS * (p + 1) + SUBLANES, :].astype(BF16)
        am_ref[0:SUBLANES, :] = am_ref[tile * SUBLANES:(tile + 1) * SUBLANES, :]


def _conv_norm_swish(even_ref, odd_ref, peven_ref, podd_ref, w_ref, cb_ref, lg_ref, lb_ref, o_ref, ym_ref):
    pairs = even_ref.shape[0]

    def operand(first):
        cur, prev, idx = ((even_ref, peven_ref, first // 2) if first % 2 == 0
                          else (odd_ref, podd_ref, (first + 1) // 2))
        return cur[idx] if idx >= 0 else prev[idx + HALO_PAIRS]

    for p0 in range(0, pairs, CONV_CHAINS):
        acc = [jnp.zeros((PACK_ROWS, LANES), F32)] * CONV_CHAINS
        for k in range(CONV_WIDTH):
            for j in range(CONV_CHAINS):
                x = operand(2 * (p0 + j) + k - (CONV_WIDTH - 1))
                acc[j] = acc[j] + w_ref[k].astype(F32) * x.astype(F32)
        for j in range(CONV_CHAINS):
            ym_ref[PACK_ROWS * (p0 + j):PACK_ROWS * (p0 + j + 1), :] = acc[j]
    for r in range(2 * pairs // NORM_ROWS):
        groups = []
        for g in range(NORM_ROWS // SUBLANES):
            t0 = r * NORM_ROWS + g * SUBLANES
            groups.append(jnp.concatenate(
                [ym_ref[pl.ds(t0 * SUBLANES + c, SUBLANES, stride=SUBLANES), :] for c in range(LANE_TILES)], axis=1))
        y = jnp.concatenate(groups, axis=0) + cb_ref[...]
        z = _layer_norm(y, lg_ref[...], lb_ref[...])
        o_ref[r * NORM_ROWS:(r + 1) * NORM_ROWS, :] = (z * jax.nn.sigmoid(z)).astype(BF16)


def _mix_kernel(h_ref, even_ref, odd_ref, peven_ref, podd_ref, cw_ref, cb_ref, clg_ref, clb_ref,
                g_ref, win_ref, bin_ref, wa_ref, slg_ref, slb_ref, sw_ref, sbias_ref,
                wb_ref, wo_ref, o_ref, ym_ref, za_ref, n_ref, v_ref, gated_ref):
    tile = h_ref.shape[0]
    _conv_norm_swish(even_ref, odd_ref, peven_ref, podd_ref, cw_ref, cb_ref, clg_ref, clb_ref, za_ref, ym_ref)
    n_ref[...] = _rms(h_ref[...], g_ref[...]).astype(BF16)

    def proj(col):
        sl = slice(col * D_MODEL, (col + 1) * D_MODEL)
        return _dot(n_ref[...], win_ref[:, sl]) + bin_ref[:, sl]

    v_ref[...] = _layer_norm(_gelu_tanh(proj(1)), slg_ref[...], slb_ref[...]).astype(BF16)
    row = lax.broadcasted_iota(jnp.int32, (CHUNK, CHUNK), 0)
    col = lax.broadcasted_iota(jnp.int32, (CHUNK, CHUNK), 1)
    w_s = [jnp.where(row >= col, sw_ref[g], 0.0).astype(BF16) for g in range(SGU_GROUPS)]
    u = _gelu_tanh(proj(0))
    for c in range(tile // CHUNK):
        rows = slice(c * CHUNK, (c + 1) * CHUNK)
        for g in range(SGU_GROUPS):
            cols = slice(g * GROUP_DIM, (g + 1) * GROUP_DIM)
            mixed = _dot(w_s[g], v_ref[rows, cols]) + sbias_ref[:, cols]
            gated_ref[rows, cols] = (u[rows, cols] * mixed).astype(BF16)
    merged = (jax.nn.sigmoid(proj(2)) * _dot(za_ref[...], wa_ref[...])
              + jax.nn.sigmoid(proj(3)) * _dot(gated_ref[...], wb_ref[...]))
    o_ref[...] = h_ref[...] + _dot(merged.astype(BF16), wo_ref[...])


def _tail_kernel(h_ref, xg_ref, wq_ref, kt_ref, v_ref, wo_ref,
                 fg_ref, wg_ref, wu_ref, wd_ref, og_ref, o_ref, attn_ref, act_ref):
    h = h_ref[...]
    q = _dot(_rms(h, xg_ref[...]).astype(BF16), wq_ref[...]).astype(BF16)
    scale = 1.0 / math.sqrt(X_HEAD_DIM)
    for hd in range(X_HEADS):
        cols = slice(hd * X_HEAD_DIM, (hd + 1) * X_HEAD_DIM)
        s = _dot(q[:, cols], kt_ref[cols, :]) * scale
        e = jnp.exp(s - jnp.max(s, axis=-1, keepdims=True))
        p = e / jnp.sum(e, axis=-1, keepdims=True)
        attn_ref[:, cols] = _dot(p.astype(BF16), v_ref[:, cols]).astype(BF16)
    h = h + _dot(attn_ref[...], wo_ref[...])
    h = _swiglu_half_step(h, fg_ref, wg_ref, wu_ref, wd_ref, act_ref)
    o_ref[...] = _rms(h, og_ref[...])


def _resident(shape):
    nd = len(shape)
    return pl.BlockSpec(shape, lambda *_: (0,) * nd, pipeline_mode=pl.Buffered(1))


def _params(semantics):
    return pltpu.CompilerParams(dimension_semantics=semantics, vmem_limit_bytes=VMEM_LIMIT_BYTES)


def _kv_call(mem, mem_norm, w_kv):
    bsz = mem.shape[0]
    return pl.pallas_call(
        _kv_kernel,
        grid=(bsz,),
        in_specs=[pl.BlockSpec((None, N_MEM, D_MODEL), lambda b: (b, 0, 0)),
                  _resident((1, D_MODEL)), _resident((D_MODEL, 2 * D_MODEL))],
        out_specs=[pl.BlockSpec((None, D_MODEL, N_MEM), lambda b: (b, 0, 0)),
                   pl.BlockSpec((None, N_MEM, D_MODEL), lambda b: (b, 0, 0))],
        out_shape=[jax.ShapeDtypeStruct((bsz, D_MODEL, N_MEM), BF16),
                   jax.ShapeDtypeStruct((bsz, N_MEM, D_MODEL), BF16)],
        compiler_params=_params(("arbitrary",)),
        name="kv_call",
    )(mem, mem_norm, w_kv)


def _ffn_glu_call(x, ffn_norm, wg, wu, wd, mix_norm, w_glu, b_glu):
    bsz, seq, _ = x.shape
    tiles = seq // TOK_TILE
    tok = pl.BlockSpec((None, TOK_TILE, D_MODEL), lambda b, j: (b, jnp.maximum(j - 1, 0), 0))
    pair_block = pl.BlockSpec((None, PAIRS, PACK_ROWS, LANES), lambda b, j: (b, j, 0, 0))
    pair_shape = jax.ShapeDtypeStruct((bsz, (tiles + 1) * PAIRS, PACK_ROWS, LANES), BF16)
    vec = _resident((1, D_MODEL))
    return pl.pallas_call(
        _ffn_glu_kernel,
        grid=(bsz, tiles + 1),
        in_specs=[tok, vec, _resident((D_MODEL, D_FF)), _resident((D_MODEL, D_FF)), _resident((D_FF, D_MODEL)),
                  vec, _resident((D_MODEL, 2 * D_MODEL)), _resident((1, 2 * D_MODEL))],
        out_specs=[tok, pair_block, pair_block],
        out_shape=[jax.ShapeDtypeStruct(x.shape, F32), pair_shape, pair_shape],
        scratch_shapes=[pltpu.VMEM((TOK_TILE, D_FF), BF16),
                        pltpu.VMEM(((TOK_TILE + 1) * SUBLANES, LANES), F32)],
        compiler_params=_params(("arbitrary", "arbitrary")),
        name="ffn_glu_call",
    )(x, ffn_norm, wg, wu, wd, mix_norm, w_glu, b_glu)


def _mix_call(h, even, odd, taps, conv_b, cln_g, cln_b, norm, w_mix, b_mix, w_a, sln_g, sln_b,
              sgu_w, sgu_bias, w_b, w_out):
    bsz, seq, _ = h.shape
    halo_per_block = PAIRS // HALO_PAIRS
    tok = pl.BlockSpec((None, TOK_TILE, D_MODEL), lambda b, j: (b, j, 0))
    cur = pl.BlockSpec((None, PAIRS, PACK_ROWS, LANES), lambda b, j: (b, j + 1, 0, 0))
    prev = pl.BlockSpec((None, HALO_PAIRS, PACK_ROWS, LANES),
                        lambda b, j: (b, (j + 1) * halo_per_block - 1, 0, 0))
    vec = _resident((1, D_MODEL))
    mat = _resident((D_MODEL, D_MODEL))
    return pl.pallas_call(
        _mix_kernel,
        grid=(bsz, seq // TOK_TILE),
        in_specs=[tok, cur, cur, prev, prev, _resident((CONV_WIDTH, PACK_ROWS, LANES)), vec, vec, vec,
                  vec, _resident((D_MODEL, 4 * D_MODEL)), _resident((1, 4 * D_MODEL)), mat,
                  vec, vec, _resident((SGU_GROUPS, CHUNK, CHUNK)), _resident((CHUNK, D_MODEL)), mat, mat],
        out_specs=tok,
        out_shape=jax.ShapeDtypeStruct(h.shape, F32),
        scratch_shapes=[pltpu.VMEM((TOK_TILE * SUBLANES, LANES), F32),
                        pltpu.VMEM((TOK_TILE, D_MODEL), BF16),
                        pltpu.VMEM((TOK_TILE, D_MODEL), BF16),
                        pltpu.VMEM((TOK_TILE, D_MODEL), BF16),
                        pltpu.VMEM((TOK_TILE, D_MODEL), BF16)],
        compiler_params=_params(("arbitrary", "arbitrary")),
        name="mix_call",
    )(h, even, odd, even, odd, taps, conv_b, cln_g, cln_b, norm, w_mix, b_mix, w_a, sln_g, sln_b,
      sgu_w, sgu_bias, w_b, w_out)


def _tail_call(h, xnorm, w_q, kt, v, w_o, fnorm, wg, wu, wd, onorm):
    bsz, seq, _ = h.shape
    tok = pl.BlockSpec((None, TOK_TILE, D_MODEL), lambda b, j: (b, j, 0))
    vec = _resident((1, D_MODEL))
    mat = _resident((D_MODEL, D_MODEL))
    return pl.pallas_call(
        _tail_kernel,
        grid=(bsz, seq // TOK_TILE),
        in_specs=[tok, vec, mat,
                  pl.BlockSpec((None, D_MODEL, N_MEM), lambda b, j: (b, 0, 0)),
                  pl.BlockSpec((None, N_MEM, D_MODEL), lambda b, j: (b, 0, 0)),
                  mat, vec, _resident((D_MODEL, D_FF)), _resident((D_MODEL, D_FF)),
                  _resident((D_FF, D_MODEL)), vec],
        out_specs=tok,
        out_shape=jax.ShapeDtypeStruct(h.shape, F32),
        scratch_shapes=[pltpu.VMEM((TOK_TILE, D_MODEL), BF16),
                        pltpu.VMEM((TOK_TILE, D_FF), BF16)],
        compiler_params=_params(("arbitrary", "arbitrary")),
        name="tail_call",
    )(h, xnorm, w_q, kt, v, w_o, fnorm, wg, wu, wd, onorm)


def kernel(x, mem, ffn1_norm, ffn1_w_gu, ffn1_w_down, mix_norm, w_in, b_in, conv_w, conv_b, conv_ln_g, conv_ln_b, w_a_out, sgu_ln_g, sgu_ln_b, sgu_w, sgu_b, w_b_out, w_out, xattn_norm, mem_norm, w_q, w_kv, w_o, ffn2_norm, ffn2_w_gu, ffn2_w_down, final_norm):
    bsz, seq, _ = x.shape
    depth = ffn1_norm.shape[0]
    assert seq % TOK_TILE == 0 and TOK_TILE % CHUNK == 0

    def vec(p):
        return p.reshape(1, -1)

    def bf(p):
        return p.astype(BF16)

    h = x
    for l in range(depth):
        h, even, odd = _ffn_glu_call(h, vec(ffn1_norm[l]), bf(ffn1_w_gu[l][:, :D_FF]), bf(ffn1_w_gu[l][:, D_FF:]),
                                     bf(ffn1_w_down[l]), vec(mix_norm[l]),
                                     bf(w_in[l][:, :2 * D_MODEL]), vec(b_in[l][:2 * D_MODEL]))
        taps = jnp.tile(bf(conv_w[l]).reshape(CONV_WIDTH, LANE_TILES, LANES), (1, 2, 1))
        sgu_bias = jnp.repeat(jnp.transpose(sgu_b[l]), GROUP_DIM, axis=1)
        h = _mix_call(h, even, odd, taps, vec(conv_b[l]), vec(conv_ln_g[l]), vec(conv_ln_b[l]),
                      vec(mix_norm[l]), bf(w_in[l][:, 2 * D_MODEL:]), vec(b_in[l][2 * D_MODEL:]),
                      bf(w_a_out[l]), vec(sgu_ln_g[l]), vec(sgu_ln_b[l]), sgu_w[l], sgu_bias,
                      bf(w_b_out[l]), bf(w_out[l]))
        kt, v = _kv_call(mem, vec(mem_norm[l]), bf(w_kv[l]))
        last = l == depth - 1
        assert last, "the final norm is fused into the last layer's tail kernel"
        h = _tail_call(h, vec(xattn_norm[l]), bf(w_q[l]), kt, v, bf(w_o[l]),
                       vec(ffn2_norm[l]), bf(ffn2_w_gu[l][:, :D_FF]), bf(ffn2_w_gu[l][:, D_FF:]),
                       bf(ffn2_w_down[l]), vec(final_norm))
    return h
```

```python
import functools
import math

import jax
import jax.numpy as jnp
from jax import lax
from jax.experimental import pallas as pl
from jax.experimental.pallas import tpu as pltpu

D_MODEL = 1024
N_MEM = 256
D_FF = 2816
CONV_WIDTH = 31
SGU_GROUPS = 4
CHUNK = 128
X_HEADS = 4
X_HEAD_DIM = D_MODEL // X_HEADS
GROUP_DIM = D_MODEL // SGU_GROUPS
D_IN = 6 * D_MODEL
EPS_RMS = 1e-6
EPS_LN = 1e-5

F32 = jnp.float32
BF16 = jnp.bfloat16

V7X_VMEM_BYTES = 64 * 1024 * 1024
VMEM_LIMIT_BYTES = V7X_VMEM_BYTES - 8 * 1024 * 1024

SUBLANES = 8
PACK_ROWS = 16
LANES = 128
LANE_TILES = D_MODEL // LANES
assert LANE_TILES == SUBLANES

TOK_TILE = 512
FF_CHUNK = 256
NORM_ROWS = 32
PAIRS = TOK_TILE // 2
HALO_PAIRS = 16
CONV_CHAINS = 4
assert PAIRS % HALO_PAIRS == 0 and PAIRS % CONV_CHAINS == 0 and 2 * (HALO_PAIRS - 1) >= CONV_WIDTH - 1


def _dot(a, b):
    return jnp.dot(a, b, preferred_element_type=F32)


def _rms(x, g):
    ms = jnp.mean(x * x, axis=-1, keepdims=True)
    return x * lax.rsqrt(ms + EPS_RMS) * g


def _layer_norm(x, g, b):
    mu = jnp.mean(x, axis=-1, keepdims=True)
    xc = x - mu
    var = jnp.mean(xc * xc, axis=-1, keepdims=True)
    return xc * lax.rsqrt(var + EPS_LN) * g + b


def _gelu_tanh(x):
    c = math.sqrt(2.0 / math.pi)
    return x * (0.5 * (1.0 + jnp.tanh(c * (x + 0.044715 * (x * x * x)))))


def _swiglu_half_step(x, g_ref, wg_ref, wu_ref, wd_ref, act_ref):
    xn = _rms(x, g_ref[...]).astype(BF16)
    for c in range(D_FF // FF_CHUNK):
        sl = slice(c * FF_CHUNK, (c + 1) * FF_CHUNK)
        gate = _dot(xn, wg_ref[:, sl])
        up = _dot(xn, wu_ref[:, sl])
        act_ref[:, sl] = (gate * jax.nn.sigmoid(gate) * up).astype(BF16)
    return x + 0.5 * _dot(act_ref[...], wd_ref[...])


def _kv_kernel(mem_ref, g_ref, wkv_ref, kt_ref, v_ref):
    memn = _rms(mem_ref[...], g_ref[...]).astype(BF16)
    kv = _dot(memn, wkv_ref[...])
    kt_ref[...] = kv[:, :D_MODEL].T.astype(BF16)
    v_ref[...] = kv[:, D_MODEL:].astype(BF16)


def _ffn_glu_kernel(x_ref, fg_ref, wg_ref, wu_ref, wd_ref, mg_ref, w_ref, b_ref,
                    h_ref, even_ref, odd_ref, act_ref, am_ref):
    tile = x_ref.shape[0]
    step = pl.program_id(1)

    @pl.when(step == 0)
    def _():
        am_ref[0:SUBLANES, :] = jnp.zeros((SUBLANES, LANES), F32)
        even_ref[...] = jnp.zeros(even_ref.shape, BF16)
        odd_ref[...] = jnp.zeros(odd_ref.shape, BF16)
        h_ref[...] = jnp.zeros(h_ref.shape, F32)

    @pl.when(step > 0)
    def _():
        h = _swiglu_half_step(x_ref[...], fg_ref, wg_ref, wu_ref, wd_ref, act_ref)
        h_ref[...] = h
        n = _rms(h, mg_ref[...]).astype(BF16)
        val = _dot(n, w_ref[:, :D_MODEL]) + b_ref[:, :D_MODEL]
        gate = _dot(n, w_ref[:, D_MODEL:]) + b_ref[:, D_MODEL:]
        a = val * jax.nn.sigmoid(gate)
        for g in range(tile // SUBLANES):
            for c in range(LANE_TILES):
                piece = a[g * SUBLANES:(g + 1) * SUBLANES, c * LANES:(c + 1) * LANES]
                start = (g * SUBLANES + 1) * SUBLANES + c
                am_ref[pl.ds(start, SUBLANES, stride=SUBLANES), :] = piece
        for p in range(tile // 2):
            odd_ref[p] = am_ref[PACK_ROWS * p:PACK_ROWS * (p + 1), :].astype(BF16)
            even_ref[p] = am_ref[PACK_ROWS * p + SUBLANES:PACK_ROWS * (p + 1) + SUBLANES, :].astype(BF16)
        am_ref[0:SUBLANES, :] = am_ref[tile * SUBLANES:(tile + 1) * SUBLANES, :]


def _conv_norm_swish(even_ref, odd_ref, peven_ref, podd_ref, w_ref, cb_ref, lg_ref, lb_ref, o_ref, ym_ref):
    pairs = even_ref.shape[0]

    def operand(first):
        cur, prev, idx = ((even_ref, peven_ref, first // 2) if first % 2 == 0
                          else (odd_ref, podd_ref, (first + 1) // 2))
        return cur[idx] if idx >= 0 else prev[idx + HALO_PAIRS]

    for p0 in range(0, pairs, CONV_CHAINS):
        acc = [jnp.zeros((PACK_ROWS, LANES), F32)] * CONV_CHAINS
        for k in range(CONV_WIDTH):
            for j in range(CONV_CHAINS):
                x = operand(2 * (p0 + j) + k - (CONV_WIDTH - 1))
                acc[j] = acc[j] + w_ref[k].astype(F32) * x.astype(F32)
        for j in range(CONV_CHAINS):
            ym_ref[PACK_ROWS * (p0 + j):PACK_ROWS * (p0 + j + 1), :] = acc[j]
    for r in range(2 * pairs // NORM_ROWS):
        groups = []
        for g in range(NORM_ROWS // SUBLANES):
            t0 = r * NORM_ROWS + g * SUBLANES
            groups.append(jnp.concatenate(
                [ym_ref[pl.ds(t0 * SUBLANES + c, SUBLANES, stride=SUBLANES), :] for c in range(LANE_TILES)], axis=1))
        y = jnp.concatenate(groups, axis=0) + cb_ref[...]
        z = _layer_norm(y, lg_ref[...], lb_ref[...])
        o_ref[r * NORM_ROWS:(r + 1) * NORM_ROWS, :] = (z * jax.nn.sigmoid(z)).astype(BF16)


def _mix_kernel(h_ref, even_ref, odd_ref, peven_ref, podd_ref, cw_ref, cb_ref, clg_ref, clb_ref,
                g_ref, wuv_ref, buv_ref, wgg_ref, bgg_ref, wa_ref, slg_ref, slb_ref, sw_ref, sbias_ref,
                wb_ref, wo_ref, o_ref, ym_ref, za_ref, n_ref, v_ref, gated_ref):
    tile = h_ref.shape[0]
    _conv_norm_swish(even_ref, odd_ref, peven_ref, podd_ref, cw_ref, cb_ref, clg_ref, clb_ref, za_ref, ym_ref)
    n_ref[...] = _rms(h_ref[...], g_ref[...]).astype(BF16)

    def proj(col):
        w_ref, b_ref = ((wuv_ref, buv_ref), (wgg_ref, bgg_ref))[col // 2]
        sl = slice((col % 2) * D_MODEL, (col % 2 + 1) * D_MODEL)
        return _dot(n_ref[...], w_ref[:, sl]) + b_ref[:, sl]

    v_ref[...] = _layer_norm(_gelu_tanh(proj(1)), slg_ref[...], slb_ref[...]).astype(BF16)
    row = lax.broadcasted_iota(jnp.int32, (CHUNK, CHUNK), 0)
    col = lax.broadcasted_iota(jnp.int32, (CHUNK, CHUNK), 1)
    w_s = [jnp.where(row >= col, sw_ref[g], 0.0).astype(BF16) for g in range(SGU_GROUPS)]
    u = _gelu_tanh(proj(0))
    for c in range(tile // CHUNK):
        rows = slice(c * CHUNK, (c + 1) * CHUNK)
        for g in range(SGU_GROUPS):
            cols = slice(g * GROUP_DIM, (g + 1) * GROUP_DIM)
            mixed = _dot(w_s[g], v_ref[rows, cols]) + sbias_ref[:, cols]
            gated_ref[rows, cols] = (u[rows, cols] * mixed).astype(BF16)
    merged = (jax.nn.sigmoid(proj(2)) * _dot(za_ref[...], wa_ref[...])
              + jax.nn.sigmoid(proj(3)) * _dot(gated_ref[...], wb_ref[...]))
    o_ref[...] = h_ref[...] + _dot(merged.astype(BF16), wo_ref[...])


def _tail_kernel(h_ref, xg_ref, wq_ref, kt_ref, v_ref, wo_ref,
                 fg_ref, wg_ref, wu_ref, wd_ref, og_ref, o_ref, attn_ref, act_ref):
    h = h_ref[...]
    q = _dot(_rms(h, xg_ref[...]).astype(BF16), wq_ref[...]).astype(BF16)
    scale = 1.0 / math.sqrt(X_HEAD_DIM)
    for hd in range(X_HEADS):
        cols = slice(hd * X_HEAD_DIM, (hd + 1) * X_HEAD_DIM)
        s = _dot(q[:, cols], kt_ref[cols, :]) * scale
        e = jnp.exp(s - jnp.max(s, axis=-1, keepdims=True))
        p = e / jnp.sum(e, axis=-1, keepdims=True)
        attn_ref[:, cols] = _dot(p.astype(BF16), v_ref[:, cols]).astype(BF16)
    h = h + _dot(attn_ref[...], wo_ref[...])
    h = _swiglu_half_step(h, fg_ref, wg_ref, wu_ref, wd_ref, act_ref)
    o_ref[...] = _rms(h, og_ref[...])


def _resident(shape):
    nd = len(shape)
    return pl.BlockSpec(shape, lambda *_: (0,) * nd, pipeline_mode=pl.Buffered(1))


def _resident_cols(rows, width, block):
    return pl.BlockSpec((rows, width), lambda *_: (0, block), pipeline_mode=pl.Buffered(1))


def _params(semantics):
    return pltpu.CompilerParams(dimension_semantics=semantics, vmem_limit_bytes=VMEM_LIMIT_BYTES)


def _kv_call(mem, mem_norm, w_kv):
    bsz = mem.shape[0]
    return pl.pallas_call(
        _kv_kernel,
        grid=(bsz,),
        in_specs=[pl.BlockSpec((None, N_MEM, D_MODEL), lambda b: (b, 0, 0)),
                  _resident((1, D_MODEL)), _resident((D_MODEL, 2 * D_MODEL))],
        out_specs=[pl.BlockSpec((None, D_MODEL, N_MEM), lambda b: (b, 0, 0)),
                   pl.BlockSpec((None, N_MEM, D_MODEL), lambda b: (b, 0, 0))],
        out_shape=[jax.ShapeDtypeStruct((bsz, D_MODEL, N_MEM), BF16),
                   jax.ShapeDtypeStruct((bsz, N_MEM, D_MODEL), BF16)],
        compiler_params=_params(("arbitrary",)),
        name="kv_call",
    )(mem, mem_norm, w_kv)


def _ffn_glu_call(x, ffn_norm, w_gu, wd, mix_norm, w_in, b_in):
    bsz, seq, _ = x.shape
    tiles = seq // TOK_TILE
    tok = pl.BlockSpec((None, TOK_TILE, D_MODEL), lambda b, j: (b, jnp.maximum(j - 1, 0), 0))
    pair_block = pl.BlockSpec((None, PAIRS, PACK_ROWS, LANES), lambda b, j: (b, j, 0, 0))
    pair_shape = jax.ShapeDtypeStruct((bsz, (tiles + 1) * PAIRS, PACK_ROWS, LANES), BF16)
    vec = _resident((1, D_MODEL))
    return pl.pallas_call(
        _ffn_glu_kernel,
        grid=(bsz, tiles + 1),
        in_specs=[tok, vec, _resident_cols(D_MODEL, D_FF, 0), _resident_cols(D_MODEL, D_FF, 1),
                  _resident((D_FF, D_MODEL)),
                  vec, _resident_cols(D_MODEL, 2 * D_MODEL, 0), _resident_cols(1, 2 * D_MODEL, 0)],
        out_specs=[tok, pair_block, pair_block],
        out_shape=[jax.ShapeDtypeStruct(x.shape, F32), pair_shape, pair_shape],
        scratch_shapes=[pltpu.VMEM((TOK_TILE, D_FF), BF16),
                        pltpu.VMEM(((TOK_TILE + 1) * SUBLANES, LANES), F32)],
        compiler_params=_params(("arbitrary", "arbitrary")),
        name="ffn_glu_call",
    )(x, ffn_norm, w_gu, w_gu, wd, mix_norm, w_in, b_in)


def _mix_call(h, even, odd, taps, conv_b, cln_g, cln_b, norm, w_in, b_in, w_a, sln_g, sln_b,
              sgu_w, sgu_bias, w_b, w_out):
    bsz, seq, _ = h.shape
    halo_per_block = PAIRS // HALO_PAIRS
    tok = pl.BlockSpec((None, TOK_TILE, D_MODEL), lambda b, j: (b, j, 0))
    cur = pl.BlockSpec((None, PAIRS, PACK_ROWS, LANES), lambda b, j: (b, j + 1, 0, 0))
    prev = pl.BlockSpec((None, HALO_PAIRS, PACK_ROWS, LANES),
                        lambda b, j: (b, (j + 1) * halo_per_block - 1, 0, 0))
    vec = _resident((1, D_MODEL))
    mat = _resident((D_MODEL, D_MODEL))
    return pl.pallas_call(
        _mix_kernel,
        grid=(bsz, seq // TOK_TILE),
        in_specs=[tok, cur, cur, prev, prev, _resident((CONV_WIDTH, PACK_ROWS, LANES)), vec, vec, vec,
                  vec, _resident_cols(D_MODEL, 2 * D_MODEL, 1), _resident_cols(1, 2 * D_MODEL, 1),
                  _resident_cols(D_MODEL, 2 * D_MODEL, 2), _resident_cols(1, 2 * D_MODEL, 2), mat,
                  vec, vec, _resident((SGU_GROUPS, CHUNK, CHUNK)), _resident((CHUNK, D_MODEL)), mat, mat],
        out_specs=tok,
        out_shape=jax.ShapeDtypeStruct(h.shape, F32),
        scratch_shapes=[pltpu.VMEM((TOK_TILE * SUBLANES, LANES), F32),
                        pltpu.VMEM((TOK_TILE, D_MODEL), BF16),
                        pltpu.VMEM((TOK_TILE, D_MODEL), BF16),
                        pltpu.VMEM((TOK_TILE, D_MODEL), BF16),
                        pltpu.VMEM((TOK_TILE, D_MODEL), BF16)],
        compiler_params=_params(("arbitrary", "arbitrary")),
        name="mix_call",
    )(h, even, odd, even, odd, taps, conv_b, cln_g, cln_b, norm, w_in, b_in, w_in, b_in, w_a, sln_g, sln_b,
      sgu_w, sgu_bias, w_b, w_out)


def _tail_call(h, xnorm, w_q, kt, v, w_o, fnorm, w_gu, wd, onorm):
    bsz, seq, _ = h.shape
    tok = pl.BlockSpec((None, TOK_TILE, D_MODEL), lambda b, j: (b, j, 0))
    vec = _resident((1, D_MODEL))
    mat = _resident((D_MODEL, D_MODEL))
    return pl.pallas_call(
        _tail_kernel,
        grid=(bsz, seq // TOK_TILE),
        in_specs=[tok, vec, mat,
                  pl.BlockSpec((None, D_MODEL, N_MEM), lambda b, j: (b, 0, 0)),
                  pl.BlockSpec((None, N_MEM, D_MODEL), lambda b, j: (b, 0, 0)),
                  mat, vec, _resident_cols(D_MODEL, D_FF, 0), _resident_cols(D_MODEL, D_FF, 1),
                  _resident((D_FF, D_MODEL)), vec],
        out_specs=tok,
        out_shape=jax.ShapeDtypeStruct(h.shape, F32),
        scratch_shapes=[pltpu.VMEM((TOK_TILE, D_MODEL), BF16),
                        pltpu.VMEM((TOK_TILE, D_FF), BF16)],
        compiler_params=_params(("arbitrary", "arbitrary")),
        name="tail_call",
    )(h, xnorm, w_q, kt, v, w_o, fnorm, w_gu, w_gu, wd, onorm)


def kernel(x, mem, ffn1_norm, ffn1_w_gu, ffn1_w_down, mix_norm, w_in, b_in, conv_w, conv_b, conv_ln_g, conv_ln_b, w_a_out, sgu_ln_g, sgu_ln_b, sgu_w, sgu_b, w_b_out, w_out, xattn_norm, mem_norm, w_q, w_kv, w_o, ffn2_norm, ffn2_w_gu, ffn2_w_down, final_norm):
    bsz, seq, _ = x.shape
    depth = ffn1_norm.shape[0]
    assert seq % TOK_TILE == 0 and TOK_TILE % CHUNK == 0

    def vec(p):
        return p.reshape(1, -1)

    def bf(p):
        return p.astype(BF16)

    h = x
    for l in range(depth):
        w_in_bf, b_in_row = bf(w_in[l]), vec(b_in[l])
        h, even, odd = _ffn_glu_call(h, vec(ffn1_norm[l]), bf(ffn1_w_gu[l]), bf(ffn1_w_down[l]),
                                     vec(mix_norm[l]), w_in_bf, b_in_row)
        taps = jnp.tile(bf(conv_w[l]).reshape(CONV_WIDTH, LANE_TILES, LANES), (1, 2, 1))
        sgu_bias = jnp.repeat(jnp.transpose(sgu_b[l]), GROUP_DIM, axis=1)
        h = _mix_call(h, even, odd, taps, vec(conv_b[l]), vec(conv_ln_g[l]), vec(conv_ln_b[l]),
                      vec(mix_norm[l]), w_in_bf, b_in_row,
                      bf(w_a_out[l]), vec(sgu_ln_g[l]), vec(sgu_ln_b[l]), sgu_w[l], sgu_bias,
                      bf(w_b_out[l]), bf(w_out[l]))
        kt, v = _kv_call(mem, vec(mem_norm[l]), bf(w_kv[l]))
        last = l == depth - 1
        assert last, "the final norm is fused into the last layer's tail kernel"
        h = _tail_call(h, vec(xattn_norm[l]), bf(w_q[l]), kt, v, bf(w_o[l]),
                       vec(ffn2_norm[l]), bf(ffn2_w_gu[l]), bf(ffn2_w_down[l]), vec(final_norm))
    return h
```

```python
import functools
import math

import jax
import jax.numpy as jnp
from jax import lax
from jax.experimental import pallas as pl
from jax.experimental.pallas import tpu as pltpu

D_MODEL = 1024
N_MEM = 256
D_FF = 2816
CONV_WIDTH = 31
SGU_GROUPS = 4
CHUNK = 128
X_HEADS = 4
X_HEAD_DIM = D_MODEL // X_HEADS
GROUP_DIM = D_MODEL // SGU_GROUPS
D_IN = 6 * D_MODEL
EPS_RMS = 1e-6
EPS_LN = 1e-5

F32 = jnp.float32
BF16 = jnp.bfloat16

V7X_VMEM_BYTES = 64 * 1024 * 1024
VMEM_LIMIT_BYTES = V7X_VMEM_BYTES - 8 * 1024 * 1024

SUBLANES = 8
PACK_ROWS = 16
LANES = 128
LANE_TILES = D_MODEL // LANES
assert LANE_TILES == SUBLANES

TOK_TILE = 512
TAIL_TILE = 1024
FF_CHUNK = 256
NORM_ROWS = 32
PAIRS = TOK_TILE // 2
HALO_PAIRS = 16
CONV_CHAINS = 4
assert PAIRS % HALO_PAIRS == 0 and PAIRS % CONV_CHAINS == 0 and 2 * (HALO_PAIRS - 1) >= CONV_WIDTH - 1


def _dot(a, b):
    return jnp.dot(a, b, preferred_element_type=F32)


def _rms(x, g):
    ms = jnp.mean(x * x, axis=-1, keepdims=True)
    return x * lax.rsqrt(ms + EPS_RMS) * g


def _layer_norm(x, g, b):
    mu = jnp.mean(x, axis=-1, keepdims=True)
    xc = x - mu
    var = jnp.mean(xc * xc, axis=-1, keepdims=True)
    return xc * lax.rsqrt(var + EPS_LN) * g + b


def _gelu_tanh(x):
    c = math.sqrt(2.0 / math.pi)
    return x * (0.5 * (1.0 + jnp.tanh(c * (x + 0.044715 * (x * x * x)))))


def _swiglu_half_step(x, g_ref, wg_ref, wu_ref, wd_ref, act_ref):
    xn = _rms(x, g_ref[...]).astype(BF16)
    for c in range(D_FF // FF_CHUNK):
        sl = slice(c * FF_CHUNK, (c + 1) * FF_CHUNK)
        gate = _dot(xn, wg_ref[:, sl])
        up = _dot(xn, wu_ref[:, sl])
        act_ref[:, sl] = (gate * jax.nn.sigmoid(gate) * up).astype(BF16)
    return x + 0.5 * _dot(act_ref[...], wd_ref[...])


def _kv_kernel(mem_ref, g_ref, wkv_ref, kt_ref, v_ref):
    memn = _rms(mem_ref[...], g_ref[...]).astype(BF16)
    kv = _dot(memn, wkv_ref[...])
    kt_ref[...] = kv[:, :D_MODEL].T.astype(BF16)
    v_ref[...] = kv[:, D_MODEL:].astype(BF16)


def _ffn_glu_kernel(x_ref, fg_ref, wg_ref, wu_ref, wd_ref, mg_ref, w_ref, b_ref,
                    h_ref, even_ref, odd_ref, act_ref, am_ref):
    tile = x_ref.shape[0]
    step = pl.program_id(1)

    @pl.when(step == 0)
    def _():
        am_ref[0:SUBLANES, :] = jnp.zeros((SUBLANES, LANES), F32)
        even_ref[...] = jnp.zeros(even_ref.shape, BF16)
        odd_ref[...] = jnp.zeros(odd_ref.shape, BF16)
        h_ref[...] = jnp.zeros(h_ref.shape, F32)

    @pl.when(step > 0)
    def _():
        h = _swiglu_half_step(x_ref[...], fg_ref, wg_ref, wu_ref, wd_ref, act_ref)
        h_ref[...] = h
        n = _rms(h, mg_ref[...]).astype(BF16)
        val = _dot(n, w_ref[:, :D_MODEL]) + b_ref[:, :D_MODEL]
        gate = _dot(n, w_ref[:, D_MODEL:]) + b_ref[:, D_MODEL:]
        a = val * jax.nn.sigmoid(gate)
        for g in range(tile // SUBLANES):
            for c in range(LANE_TILES):
                piece = a[g * SUBLANES:(g + 1) * SUBLANES, c * LANES:(c + 1) * LANES]
                start = (g * SUBLANES + 1) * SUBLANES + c
                am_ref[pl.ds(start, SUBLANES, stride=SUBLANES), :] = piece
        for p in range(tile // 2):
            odd_ref[p] = am_ref[PACK_ROWS * p:PACK_ROWS * (p + 1), :].astype(BF16)
            even_ref[p] = am_ref[PACK_ROWS * p + SUBLANES:PACK_ROWS * (p + 1) + SUBLANES, :].astype(BF16)
        am_ref[0:SUBLANES, :] = am_ref[tile * SUBLANES:(tile + 1) * SUBLANES, :]


def _conv_norm_swish(even_ref, odd_ref, peven_ref, podd_ref, w_ref, cb_ref, lg_ref, lb_ref, o_ref, ym_ref):
    pairs = even_ref.shape[0]

    def operand(first):
        cur, prev, idx = ((even_ref, peven_ref, first // 2) if first % 2 == 0
                          else (odd_ref, podd_ref, (first + 1) // 2))
        return cur[idx] if idx >= 0 else prev[idx + HALO_PAIRS]

    for p0 in range(0, pairs, CONV_CHAINS):
        acc = [jnp.zeros((PACK_ROWS, LANES), F32)] * CONV_CHAINS
        for k in range(CONV_WIDTH):
            for j in range(CONV_CHAINS):
                x = operand(2 * (p0 + j) + k - (CONV_WIDTH - 1))
                acc[j] = acc[j] + w_ref[k].astype(F32) * x.astype(F32)
        for j in range(CONV_CHAINS):
            ym_ref[PACK_ROWS * (p0 + j):PACK_ROWS * (p0 + j + 1), :] = acc[j]
    for r in range(2 * pairs // NORM_ROWS):
        groups = []
        for g in range(NORM_ROWS // SUBLANES):
            t0 = r * NORM_ROWS + g * SUBLANES
            groups.append(jnp.concatenate(
                [ym_ref[pl.ds(t0 * SUBLANES + c, SUBLANES, stride=SUBLANES), :] for c in range(LANE_TILES)], axis=1))
        y = jnp.concatenate(groups, axis=0) + cb_ref[...]
        z = _layer_norm(y, lg_ref[...], lb_ref[...])
        o_ref[r * NORM_ROWS:(r + 1) * NORM_ROWS, :] = (z * jax.nn.sigmoid(z)).astype(BF16)


def _mix_kernel(h_ref, even_ref, odd_ref, peven_ref, podd_ref, cw_ref, cb_ref, clg_ref, clb_ref,
                g_ref, wuv_ref, buv_ref, wgg_ref, bgg_ref, wa_ref, slg_ref, slb_ref, sw_ref, sbias_ref,
                wb_ref, wo_ref, o_ref, ym_ref, za_ref, n_ref, v_ref, gated_ref):
    tile = h_ref.shape[0]
    _conv_norm_swish(even_ref, odd_ref, peven_ref, podd_ref, cw_ref, cb_ref, clg_ref, clb_ref, za_ref, ym_ref)
    n_ref[...] = _rms(h_ref[...], g_ref[...]).astype(BF16)

    def proj(col):
        w_ref, b_ref = ((wuv_ref, buv_ref), (wgg_ref, bgg_ref))[col // 2]
        sl = slice((col % 2) * D_MODEL, (col % 2 + 1) * D_MODEL)
        return _dot(n_ref[...], w_ref[:, sl]) + b_ref[:, sl]

    v_ref[...] = _layer_norm(_gelu_tanh(proj(1)), slg_ref[...], slb_ref[...]).astype(BF16)
    row = lax.broadcasted_iota(jnp.int32, (CHUNK, CHUNK), 0)
    col = lax.broadcasted_iota(jnp.int32, (CHUNK, CHUNK), 1)
    w_s = [jnp.where(row >= col, sw_ref[g], 0.0).astype(BF16) for g in range(SGU_GROUPS)]
    u = _gelu_tanh(proj(0))
    for c in range(tile // CHUNK):
        rows = slice(c * CHUNK, (c + 1) * CHUNK)
        for g in range(SGU_GROUPS):
            cols = slice(g * GROUP_DIM, (g + 1) * GROUP_DIM)
            mixed = _dot(w_s[g], v_ref[rows, cols]) + sbias_ref[:, cols]
            gated_ref[rows, cols] = (u[rows, cols] * mixed).astype(BF16)
    merged = (jax.nn.sigmoid(proj(2)) * _dot(za_ref[...], wa_ref[...])
              + jax.nn.sigmoid(proj(3)) * _dot(gated_ref[...], wb_ref[...]))
    o_ref[...] = h_ref[...] + _dot(merged.astype(BF16), wo_ref[...])


def _tail_kernel(h_ref, xg_ref, wq_ref, kt_ref, v_ref, wo_ref,
                 fg_ref, wg_ref, wu_ref, wd_ref, og_ref, o_ref, attn_ref, act_ref):
    h = h_ref[...]
    q = _dot(_rms(h, xg_ref[...]).astype(BF16), wq_ref[...]).astype(BF16)
    scale = 1.0 / math.sqrt(X_HEAD_DIM)
    for hd in range(X_HEADS):
        cols = slice(hd * X_HEAD_DIM, (hd + 1) * X_HEAD_DIM)
        s = _dot(q[:, cols], kt_ref[cols, :]) * scale
        e = jnp.exp(s - jnp.max(s, axis=-1, keepdims=True))
        p = e / jnp.sum(e, axis=-1, keepdims=True)
        attn_ref[:, cols] = _dot(p.astype(BF16), v_ref[:, cols]).astype(BF16)
    h = h + _dot(attn_ref[...], wo_ref[...])
    h = _swiglu_half_step(h, fg_ref, wg_ref, wu_ref, wd_ref, act_ref)
    o_ref[...] = _rms(h, og_ref[...])


def _resident(shape):
    nd = len(shape)
    return pl.BlockSpec(shape, lambda *_: (0,) * nd, pipeline_mode=pl.Buffered(1))


def _resident_cols(rows, width, block):
    return pl.BlockSpec((rows, width), lambda *_: (0, block), pipeline_mode=pl.Buffered(1))


def _params(semantics):
    return pltpu.CompilerParams(dimension_semantics=semantics, vmem_limit_bytes=VMEM_LIMIT_BYTES)


def _kv_call(mem, mem_norm, w_kv):
    bsz = mem.shape[0]
    return pl.pallas_call(
        _kv_kernel,
        grid=(bsz,),
        in_specs=[pl.BlockSpec((None, N_MEM, D_MODEL), lambda b: (b, 0, 0)),
                  _resident((1, D_MODEL)), _resident((D_MODEL, 2 * D_MODEL))],
        out_specs=[pl.BlockSpec((None, D_MODEL, N_MEM), lambda b: (b, 0, 0)),
                   pl.BlockSpec((None, N_MEM, D_MODEL), lambda b: (b, 0, 0))],
        out_shape=[jax.ShapeDtypeStruct((bsz, D_MODEL, N_MEM), BF16),
                   jax.ShapeDtypeStruct((bsz, N_MEM, D_MODEL), BF16)],
        compiler_params=_params(("arbitrary",)),
        name="kv_call",
    )(mem, mem_norm, w_kv)


def _ffn_glu_call(x, ffn_norm, w_gu, wd, mix_norm, w_in, b_in):
    bsz, seq, _ = x.shape
    tiles = seq // TOK_TILE
    tok = pl.BlockSpec((None, TOK_TILE, D_MODEL), lambda b, j: (b, jnp.maximum(j - 1, 0), 0))
    pair_block = pl.BlockSpec((None, PAIRS, PACK_ROWS, LANES), lambda b, j: (b, j, 0, 0))
    pair_shape = jax.ShapeDtypeStruct((bsz, (tiles + 1) * PAIRS, PACK_ROWS, LANES), BF16)
    vec = _resident((1, D_MODEL))
    return pl.pallas_call(
        _ffn_glu_kernel,
        grid=(bsz, tiles + 1),
        in_specs=[tok, vec, _resident_cols(D_MODEL, D_FF, 0), _resident_cols(D_MODEL, D_FF, 1),
                  _resident((D_FF, D_MODEL)),
                  vec, _resident_cols(D_MODEL, 2 * D_MODEL, 0), _resident_cols(1, 2 * D_MODEL, 0)],
        out_specs=[tok, pair_block, pair_block],
        out_shape=[jax.ShapeDtypeStruct(x.shape, F32), pair_shape, pair_shape],
        scratch_shapes=[pltpu.VMEM((TOK_TILE, D_FF), BF16),
                        pltpu.VMEM(((TOK_TILE + 1) * SUBLANES, LANES), F32)],
        compiler_params=_params(("arbitrary", "arbitrary")),
        name="ffn_glu_call",
    )(x, ffn_norm, w_gu, w_gu, wd, mix_norm, w_in, b_in)


def _mix_call(h, even, odd, taps, conv_b, cln_g, cln_b, norm, w_in, b_in, w_a, sln_g, sln_b,
              sgu_w, sgu_bias, w_b, w_out):
    bsz, seq, _ = h.shape
    halo_per_block = PAIRS // HALO_PAIRS
    tok = pl.BlockSpec((None, TOK_TILE, D_MODEL), lambda b, j: (b, j, 0))
    cur = pl.BlockSpec((None, PAIRS, PACK_ROWS, LANES), lambda b, j: (b, j + 1, 0, 0))
    prev = pl.BlockSpec((None, HALO_PAIRS, PACK_ROWS, LANES),
                        lambda b, j: (b, (j + 1) * halo_per_block - 1, 0, 0))
    vec = _resident((1, D_MODEL))
    mat = _resident((D_MODEL, D_MODEL))
    return pl.pallas_call(
        _mix_kernel,
        grid=(bsz, seq // TOK_TILE),
        in_specs=[tok, cur, cur, prev, prev, _resident((CONV_WIDTH, PACK_ROWS, LANES)), vec, vec, vec,
                  vec, _resident_cols(D_MODEL, 2 * D_MODEL, 1), _resident_cols(1, 2 * D_MODEL, 1),
                  _resident_cols(D_MODEL, 2 * D_MODEL, 2), _resident_cols(1, 2 * D_MODEL, 2), mat,
                  vec, vec, _resident((SGU_GROUPS, CHUNK, CHUNK)), _resident((CHUNK, D_MODEL)), mat, mat],
        out_specs=tok,
        out_shape=jax.ShapeDtypeStruct(h.shape, F32),
        scratch_shapes=[pltpu.VMEM((TOK_TILE * SUBLANES, LANES), F32),
                        pltpu.VMEM((TOK_TILE, D_MODEL), BF16),
                        pltpu.VMEM((TOK_TILE, D_MODEL), BF16),
                        pltpu.VMEM((TOK_TILE, D_MODEL), BF16),
                        pltpu.VMEM((TOK_TILE, D_MODEL), BF16)],
        compiler_params=_params(("arbitrary", "arbitrary")),
        name="mix_call",
    )(h, even, odd, even, odd, taps, conv_b, cln_g, cln_b, norm, w_in, b_in, w_in, b_in, w_a, sln_g, sln_b,
      sgu_w, sgu_bias, w_b, w_out)


def _tail_call(h, xnorm, w_q, kt, v, w_o, fnorm, w_gu, wd, onorm):
    bsz, seq, _ = h.shape
    tok = pl.BlockSpec((None, TAIL_TILE, D_MODEL), lambda b, j: (b, j, 0))
    vec = _resident((1, D_MODEL))
    mat = _resident((D_MODEL, D_MODEL))
    return pl.pallas_call(
        _tail_kernel,
        grid=(bsz, seq // TAIL_TILE),
        in_specs=[tok, vec, mat,
                  pl.BlockSpec((None, D_MODEL, N_MEM), lambda b, j: (b, 0, 0)),
                  pl.BlockSpec((None, N_MEM, D_MODEL), lambda b, j: (b, 0, 0)),
                  mat, vec, _resident_cols(D_MODEL, D_FF, 0), _resident_cols(D_MODEL, D_FF, 1),
                  _resident((D_FF, D_MODEL)), vec],
        out_specs=tok,
        out_shape=jax.ShapeDtypeStruct(h.shape, F32),
        scratch_shapes=[pltpu.VMEM((TAIL_TILE, D_MODEL), BF16),
                        pltpu.VMEM((TAIL_TILE, D_FF), BF16)],
        compiler_params=_params(("arbitrary", "arbitrary")),
        name="tail_call",
    )(h, xnorm, w_q, kt, v, w_o, fnorm, w_gu, w_gu, wd, onorm)


def kernel(x, mem, ffn1_norm, ffn1_w_gu, ffn1_w_down, mix_norm, w_in, b_in, conv_w, conv_b, conv_ln_g, conv_ln_b, w_a_out, sgu_ln_g, sgu_ln_b, sgu_w, sgu_b, w_b_out, w_out, xattn_norm, mem_norm, w_q, w_kv, w_o, ffn2_norm, ffn2_w_gu, ffn2_w_down, final_norm):
    bsz, seq, _ = x.shape
    depth = ffn1_norm.shape[0]
    assert seq % TOK_TILE == 0 and seq % TAIL_TILE == 0 and TOK_TILE % CHUNK == 0

    def vec(p):
        return p.reshape(1, -1)

    def bf(p):
        return p.astype(BF16)

    h = x
    for l in range(depth):
        w_in_bf, b_in_row = bf(w_in[l]), vec(b_in[l])
        h, even, odd = _ffn_glu_call(h, vec(ffn1_norm[l]), bf(ffn1_w_gu[l]), bf(ffn1_w_down[l]),
                                     vec(mix_norm[l]), w_in_bf, b_in_row)
        taps = jnp.tile(bf(conv_w[l]).reshape(CONV_WIDTH, LANE_TILES, LANES), (1, 2, 1))
        sgu_bias = jnp.repeat(jnp.transpose(sgu_b[l]), GROUP_DIM, axis=1)
        h = _mix_call(h, even, odd, taps, vec(conv_b[l]), vec(conv_ln_g[l]), vec(conv_ln_b[l]),
                      vec(mix_norm[l]), w_in_bf, b_in_row,
                      bf(w_a_out[l]), vec(sgu_ln_g[l]), vec(sgu_ln_b[l]), sgu_w[l], sgu_bias,
                      bf(w_b_out[l]), bf(w_out[l]))
        kt, v = _kv_call(mem, vec(mem_norm[l]), bf(w_kv[l]))
        last = l == depth - 1
        assert last, "the final norm is fused into the last layer's tail kernel"
        h = _tail_call(h, vec(xattn_norm[l]), bf(w_q[l]), kt, v, bf(w_o[l]),
                       vec(ffn2_norm[l]), bf(ffn2_w_gu[l]), bf(ffn2_w_down[l]), vec(final_norm))
    return h
```
